```python
import jax, jax.numpy as jnp
from jax import lax
import numpy as np

D_MODEL = 2048
BATCH = 32
SEQ = 256
DEPTH = 2
DEC_BATCH = 2
DEC_SEQ = 1024
PAST_LEN = 512

GRID_W = 64
BLOCK_Q = 128
ROPE_THETA = 10000.0
EPS = 1e-6
GQA_HEADS = 6
GQA_KV_HEADS = 2
GQA_HEAD_DIM = 128
CONV_CH = 512
CONV_W = 3
MLA_HEADS = 6
MLA_Q_RANK = 512
MLA_KV_RANK = 256
MLA_NOPE = 128
MLA_ROPE = 64
MLA_V = 128
GQA_WIDTH = GQA_HEADS * GQA_HEAD_DIM
MLA_WIDTH = MLA_HEADS * MLA_V
D_MIX = GQA_WIDTH + CONV_CH + MLA_WIDTH
IN_SIZES = (GQA_HEADS * GQA_HEAD_DIM, GQA_KV_HEADS * GQA_HEAD_DIM, GQA_KV_HEADS * GQA_HEAD_DIM,
            CONV_CH, CONV_CH, CONV_CH, MLA_Q_RANK, MLA_KV_RANK, MLA_ROPE)
IN_WIDTH = sum(IN_SIZES)
N_EXPERTS = 64
TOP_K = 8
N_GROUPS = 8
TOPK_GROUPS = 4
EXPERTS_PER_GROUP = N_EXPERTS // N_GROUPS
D_EXPERT = 256
D_SHARED = 256
ROUTED_SCALE = 2.5

kernel_name = 'hybrid_prefix_diffusion_step'


def rms_norm(x, g):
    xf = x.astype(jnp.float32)
    y = xf * lax.rsqrt(jnp.mean(xf * xf, axis=-1, keepdims=True) + EPS)
    return (y * g.astype(jnp.float32)).astype(x.dtype)


def split_last(z, sizes):
    cuts = []
    acc = 0
    for s in sizes[:-1]:
        acc += s
        cuts.append(acc)
    return jnp.split(z, cuts, axis=-1)


def grid_positions(n_tokens):
    rows = n_tokens // GRID_W
    row = jnp.repeat(jnp.arange(rows, dtype=jnp.int32), GRID_W)
    col = jnp.tile(jnp.arange(GRID_W, dtype=jnp.int32), rows)
    return row, col


def axial_rope_tables(row, col, dim):
    half = dim // 2
    inv_freq = ROPE_THETA ** (-(jnp.arange(half // 2, dtype=jnp.float32) * 2.0 / half))
    ang_r = row.astype(jnp.float32)[:, None] * inv_freq[None, :]
    ang_c = col.astype(jnp.float32)[:, None] * inv_freq[None, :]
    return (jnp.cos(ang_r), jnp.sin(ang_r), jnp.cos(ang_c), jnp.sin(ang_c))


def rotate(x, cos, sin):
    d = x.shape[-1] // 2
    x1, x2 = x[..., :d], x[..., d:]
    cos = cos[None, :, None, :]
    sin = sin[None, :, None, :]
    return jnp.concatenate([x1 * cos - x2 * sin, x2 * cos + x1 * sin], axis=-1)


def apply_axial_rope(x, tables):
    cos_r, sin_r, cos_c, sin_c = tables
    xf = x.astype(jnp.float32)
    half = x.shape[-1] // 2
    out = jnp.concatenate([rotate(xf[..., :half], cos_r, sin_r),
                           rotate(xf[..., half:], cos_c, sin_c)], axis=-1)
    return out.astype(x.dtype)


def block_attention(q, k, v):
    b, sq, h, dh = q.shape
    hk = k.shape[2]
    g = h // hk
    dv = v.shape[-1]
    nb = sq // BLOCK_Q
    scale = dh ** -0.5
    qb = q.reshape(b, nb, BLOCK_Q, hk, g, dh).transpose(1, 0, 2, 3, 4, 5)

    def one_block(qi):
        s = jnp.einsum('bqhgd,bkhd->bhgqk', qi, k, preferred_element_type=jnp.float32) * scale
        p = jax.nn.softmax(s, axis=-1)
        return jnp.einsum('bhgqk,bkhd->bqhgd', p.astype(v.dtype), v)

    ob = lax.map(one_block, qb)
    return ob.transpose(1, 0, 2, 3, 4, 5).reshape(b, sq, h * dv)


def short_conv(u, w):
    return lax.conv_general_dilated(u, w[:, None, :], window_strides=(1,),
                                    padding=((CONV_W // 2, CONV_W // 2),),
                                    dimension_numbers=('NWC', 'WIO', 'NWC'),
                                    feature_group_count=u.shape[-1])


def token_mixers(h, p, ctx=None, rope=None):
    b, s, _ = h.shape
    qg, kg, vg, cb, cc, ch, cq, ckv, kr = split_last(h @ p['w_in'], IN_SIZES)
    q = rms_norm(qg.reshape(b, s, GQA_HEADS, GQA_HEAD_DIM), p['g_q'])
    k = rms_norm(kg.reshape(b, s, GQA_KV_HEADS, GQA_HEAD_DIM), p['g_k'])
    v = vg.reshape(b, s, GQA_KV_HEADS, GQA_HEAD_DIM)
    q_m = (rms_norm(cq, p['g_mla_q']) @ p['w_uq']).reshape(b, s, MLA_HEADS, MLA_NOPE + MLA_ROPE)
    q_nope, q_pe = q_m[..., :MLA_NOPE], q_m[..., MLA_NOPE:]
    ckv_n = rms_norm(ckv, p['g_mla_kv'])
    k_pe = kr
    if rope is not None:
        rope_a, rope_c = rope
        q = apply_axial_rope(q, rope_a)
        k = apply_axial_rope(k, rope_a)
        q_pe = apply_axial_rope(q_pe, rope_c)
        k_pe = apply_axial_rope(kr[:, :, None, :], rope_c)[:, :, 0, :]
    new_ctx = (k, v, ckv_n, k_pe)
    if ctx is None:
        k_all, v_all, ckv_all, kpe_all = new_ctx
    else:
        ctx_k, ctx_v, ctx_ckv, ctx_kpe = ctx
        k_all = jnp.concatenate([ctx_k, k], axis=1)
        v_all = jnp.concatenate([ctx_v, v], axis=1)
        ckv_all = jnp.concatenate([ctx_ckv, ckv_n], axis=1)
        kpe_all = jnp.concatenate([ctx_kpe, k_pe], axis=1)
    out_a = block_attention(q, k_all, v_all)
    kv_m = (ckv_all @ p['w_ukv']).reshape(b, -1, MLA_HEADS, MLA_NOPE + MLA_V)
    k_m = jnp.concatenate([kv_m[..., :MLA_NOPE],
                           jnp.broadcast_to(kpe_all[:, :, None, :], kv_m.shape[:3] + (MLA_ROPE,))], axis=-1)
    v_m = kv_m[..., MLA_NOPE:]
    out_c = block_attention(jnp.concatenate([q_nope, q_pe], axis=-1), k_m, v_m)
    out_b = cb * short_conv(cc * ch, p['conv_w'])
    g_a, g_b, g_c = split_last(p['g_grp'], (GQA_WIDTH, CONV_CH, MLA_WIDTH))
    merged = jnp.concatenate([rms_norm(out_a, g_a), rms_norm(out_b, g_b), rms_norm(out_c, g_c)], axis=-1)
    return merged @ p['w_out'], new_ctx


def moe_ffn(h, p):
    b, s, d = h.shape
    x = h.reshape(b * s, d)
    t = x.shape[0]
    scores = jax.nn.sigmoid((x @ p['w_router']).astype(jnp.float32))
    choice = scores + p['b_router'].astype(jnp.float32)
    grouped = choice.reshape(t, N_GROUPS, EXPERTS_PER_GROUP)
    group_score = jnp.sum(lax.top_k(grouped, 2)[0], axis=-1)
    _, g_idx = lax.top_k(group_score, TOPK_GROUPS)
    g_mask = jnp.sum(jax.nn.one_hot(g_idx, N_GROUPS, dtype=jnp.float32), axis=1)
    e_mask = jnp.repeat(g_mask, EXPERTS_PER_GROUP, axis=1) > 0
    _, e_idx = lax.top_k(jnp.where(e_mask, choice, -jnp.inf), TOP_K)
    w = jnp.take_along_axis(scores, e_idx, axis=-1)
    w = w / jnp.sum(w, axis=-1, keepdims=True) * ROUTED_SCALE
    combine = jnp.sum(jax.nn.one_hot(e_idx, N_EXPERTS, dtype=jnp.float32) * w[..., None],
                      axis=1).astype(x.dtype)
    routed = jnp.zeros_like(x)
    for gi in range(N_GROUPS):
        sl = slice(gi * EXPERTS_PER_GROUP, (gi + 1) * EXPERTS_PER_GROUP)
        hg = jnp.einsum('td,edf->tef', x, p['w_gate'][sl])
        hu = jnp.einsum('td,edf->tef', x, p['w_up'][sl])
        a = jax.nn.silu(hg) * hu * combine[:, sl, None]
        routed = routed + jnp.einsum('tef,efd->td', a, p['w_down'][sl])
    shared = (jax.nn.silu(x @ p['ws_gate']) * (x @ p['ws_up'])) @ p['ws_down']
    return (routed + shared).reshape(b, s, d)


def modulation(cond, w_mod, b_mod):
    m = jax.nn.silu(cond) @ w_mod + b_mod
    return jnp.split(m[:, None, :], 6, axis=-1)


def trunk_layer(x, mod, p, ctx=None, rope=None):
    sh1, sc1, gt1, sh2, sc2, gt2 = mod
    h = rms_norm(x, p['g_mix']) * (1.0 + sc1) + sh1
    m, new_ctx = token_mixers(h, p, ctx, rope)
    x = x + gt1 * m
    h = rms_norm(x, p['g_ffn']) * (1.0 + sc2) + sh2
    x = x + gt2 * moe_ffn(h, p)
    return x, new_ctx


def setup_inputs(seed: int = 0) -> dict:
    key = jax.random.key(seed)
    ks = jax.random.split(key, 32)

    def nrm(k, shape, scale=1.0):
        return jax.random.normal(k, shape, dtype=jnp.float32) * scale

    def gain(k, shape):
        return 1.0 + 0.05 * jax.random.normal(k, shape, dtype=jnp.float32)

    L = DEPTH
    return {
        'x_prompt': nrm(ks[0], (BATCH, SEQ, D_MODEL)),
        'x_sample': nrm(ks[1], (DEC_BATCH, DEC_SEQ, D_MODEL)),
        'c': nrm(ks[2], (DEC_BATCH, D_MODEL)),
        'cache_gqa_k': nrm(ks[3], (DEC_BATCH, L, PAST_LEN, GQA_KV_HEADS, GQA_HEAD_DIM)),
        'cache_gqa_v': nrm(ks[4], (DEC_BATCH, L, PAST_LEN, GQA_KV_HEADS, GQA_HEAD_DIM)),
        'cache_mla_ckv': nrm(ks[5], (DEC_BATCH, L, PAST_LEN, MLA_KV_RANK)),
        'cache_mla_kpe': nrm(ks[6], (DEC_BATCH, L, PAST_LEN, MLA_ROPE)),
        'c_ctx': nrm(ks[7], (D_MODEL,)),
        'w_mod': nrm(ks[8], (L, D_MODEL, 6 * D_MODEL), 0.5 * D_MODEL ** -0.5),
        'b_mod': nrm(ks[9], (L, 6 * D_MODEL), 0.02),
        'g_mix': gain(ks[10], (L, D_MODEL)),
        'w_in': nrm(ks[11], (L, D_MODEL, IN_WIDTH), D_MODEL ** -0.5),
        'g_q': gain(ks[12], (L, GQA_HEAD_DIM)),
        'g_k': gain(ks[13], (L, GQA_HEAD_DIM)),
        'conv_w': nrm(ks[14], (L, CONV_W, CONV_CH), CONV_W ** -0.5),
        'g_mla_q': gain(ks[15], (L, MLA_Q_RANK)),
        'g_mla_kv': gain(ks[16], (L, MLA_KV_RANK)),
        'w_uq': nrm(ks[17], (L, MLA_Q_RANK, MLA_HEADS * (MLA_NOPE + MLA_ROPE)), MLA_Q_RANK ** -0.5),
        'w_ukv': nrm(ks[18], (L, MLA_KV_RANK, MLA_HEADS * (MLA_NOPE + MLA_V)), MLA_KV_RANK ** -0.5),
        'g_grp': gain(ks[19], (L, D_MIX)),
        'w_out': nrm(ks[20], (L, D_MIX, D_MODEL), D_MIX ** -0.5),
        'g_ffn': gain(ks[21], (L, D_MODEL)),
        'w_router': nrm(ks[22], (L, D_MODEL, N_EXPERTS), D_MODEL ** -0.5),
        'b_router': nrm(ks[23], (L, N_EXPERTS), 0.01),
        'w_gate': nrm(ks[24], (L, N_EXPERTS, D_MODEL, D_EXPERT), D_MODEL ** -0.5),
        'w_up': nrm(ks[25], (L, N_EXPERTS, D_MODEL, D_EXPERT), D_MODEL ** -0.5),
        'w_down': nrm(ks[26], (L, N_EXPERTS, D_EXPERT, D_MODEL), D_EXPERT ** -0.5),
        'ws_gate': nrm(ks[27], (L, D_MODEL, D_SHARED), D_MODEL ** -0.5),
        'ws_up': nrm(ks[28], (L, D_MODEL, D_SHARED), D_MODEL ** -0.5),
        'ws_down': nrm(ks[29], (L, D_SHARED, D_MODEL), D_SHARED ** -0.5),
        'g_final': gain(ks[30], (D_MODEL,)),
    }


def reference(x_prompt, x_sample, c, cache_gqa_k, cache_gqa_v, cache_mla_ckv, cache_mla_kpe, c_ctx,
              w_mod, b_mod, g_mix, w_in, g_q, g_k, conv_w, g_mla_q, g_mla_kv, w_uq, w_ukv, g_grp,
              w_out, g_ffn, w_router, b_router, w_gate, w_up, w_down, ws_gate, ws_up, ws_down, g_final):
    row, col = grid_positions(x_sample.shape[1])
    rope = (axial_rope_tables(row, col, GQA_HEAD_DIM), axial_rope_tables(row, col, MLA_ROPE))
    yp = x_prompt
    ys = x_sample
    ks_, vs_, ckvs_, kpes_ = [], [], [], []
    for l in range(DEPTH):
        p = {'w_in': w_in[l], 'g_q': g_q[l], 'g_k': g_k[l], 'conv_w': conv_w[l],
             'g_mla_q': g_mla_q[l], 'g_mla_kv': g_mla_kv[l], 'w_uq': w_uq[l], 'w_ukv': w_ukv[l],
             'g_grp': g_grp[l], 'w_out': w_out[l], 'g_mix': g_mix[l], 'g_ffn': g_ffn[l],
             'w_router': w_router[l], 'b_router': b_router[l], 'w_gate': w_gate[l],
             'w_up': w_up[l], 'w_down': w_down[l], 'ws_gate': ws_gate[l], 'ws_up': ws_up[l],
             'ws_down': ws_down[l]}
        mod_ctx = modulation(c_ctx[None, :], w_mod[l], b_mod[l])
        mod_lat = modulation(c, w_mod[l], b_mod[l])
        yp, (ck, cv, cckv, ckpe) = trunk_layer(yp, mod_ctx, p)
        ks_.append(ck)
        vs_.append(cv)
        ckvs_.append(cckv)
        kpes_.append(ckpe)
        cached = (cache_gqa_k[:, l], cache_gqa_v[:, l], cache_mla_ckv[:, l], cache_mla_kpe[:, l])
        ys, _ = trunk_layer(ys, mod_lat, p, ctx=cached, rope=rope)
    y_prompt = rms_norm(yp, g_final)
    y_sample = rms_norm(ys, g_final)
    new_gqa_k = jnp.stack(ks_, axis=1)
    new_gqa_v = jnp.stack(vs_, axis=1)
    new_mla_ckv = jnp.stack(ckvs_, axis=1)
    new_mla_kpe = jnp.stack(kpes_, axis=1)
    return (y_prompt, y_sample, new_gqa_k, new_gqa_v, new_mla_ckv, new_mla_kpe)
```

```python
import functools

import jax
import jax.numpy as jnp
from jax import lax
from jax.experimental import pallas as pl
from jax.experimental.pallas import tpu as pltpu

F32 = jnp.float32
BF16 = jnp.bfloat16

EPS = 1e-6
ROPE_THETA = 10000.0
GRID_W = 64
GQA_HEADS = 6
GQA_KV_HEADS = 2
HEAD_DIM = 128
CONV_CH = 512
MLA_HEADS = 6
MLA_Q_RANK = 512
MLA_KV_RANK = 256
MLA_NOPE = 128
MLA_ROPE = 64
MLA_V = 128
GQA_WIDTH = GQA_HEADS * HEAD_DIM
MLA_WIDTH = MLA_HEADS * MLA_V
N_EXPERTS = 64
TOP_K = 8
N_GROUPS = 8
TOPK_GROUPS = 4
GROUP_SIZE = N_EXPERTS // N_GROUPS
ROUTED_SCALE = 2.5

LANES = 128
VMEM_LIMIT_BYTES = 56 * 1024 * 1024

ZQ_W = GQA_WIDTH + MLA_Q_RANK + CONV_CH
KV_W = GQA_KV_HEADS * HEAD_DIM
ZS_W = 2 * KV_W + MLA_KV_RANK + 2 * CONV_CH + LANES
ZS_K, ZS_V, ZS_CKV = 0, KV_W, 2 * KV_W
ZS_CC = ZS_CKV + MLA_KV_RANK
ZS_CH = ZS_CC + CONV_CH
ZS_KR = ZS_CH + CONV_CH
CONV_HALO = 8
MLA_QH = 2 * LANES
MLA_KVH = MLA_NOPE + MLA_V


def _cparams(sem, vmem=VMEM_LIMIT_BYTES):
    return pltpu.CompilerParams(dimension_semantics=sem, vmem_limit_bytes=vmem)


def _resident(shape, index_map):
    return pl.BlockSpec(shape, index_map, pipeline_mode=pl.Buffered(1))


def _rms(x, g):
    ms = jnp.mean(x * x, axis=-1, keepdims=True)
    return x * lax.rsqrt(ms + EPS) * g


def _dot(a, b):
    return jnp.dot(a, b, preferred_element_type=F32)


def _dot_nt(a, b):
    return lax.dot_general(a, b, (((1,), (1,)), ((), ())), preferred_element_type=F32)


def _rope(x, cos, sin_signed, hb):
    lane = lax.broadcasted_iota(jnp.int32, x.shape, 1)
    partner = jnp.where((lane % (2 * hb)) < hb,
                        pltpu.roll(x, LANES - hb, axis=1), pltpu.roll(x, hb, axis=1))
    return x * cos + partner * sin_signed


def _mod_kernel(c_ref, w_ref, b_ref, o_ref):
    c = c_ref[...]
    a = (c * jax.nn.sigmoid(c)).astype(BF16)
    o_ref[...] = _dot(a, w_ref[...].astype(BF16)) + b_ref[...]


def _modulation(cond, w_mod, b_mod, tn=1024):
    depth, d, n = w_mod.shape
    rows = cond.shape[0]
    return pl.pallas_call(
        _mod_kernel,
        grid=(depth, n // tn),
        in_specs=[
            pl.BlockSpec((rows, d), lambda l, j: (0, 0)),
            pl.BlockSpec((None, d, tn), lambda l, j: (l, 0, j)),
            pl.BlockSpec((None, 1, tn), lambda l, j: (l, 0, j)),
        ],
        out_specs=pl.BlockSpec((None, rows, tn), lambda l, j: (l, 0, j)),
        out_shape=jax.ShapeDtypeStruct((depth, rows, n), F32),
        compiler_params=_cparams(("parallel", "parallel")),
        name="modulation",
    )(cond, w_mod, b_mod.reshape(depth, 1, n))


def _in_proj_kernel(x_ref, g_ref, sh_ref, sc_ref, w_ref, zq_ref, zs_ref):
    h = _rms(x_ref[...], g_ref[...]) * (1.0 + sc_ref[...]) + sh_ref[...]
    z = _dot(h.astype(BF16), w_ref[...])
    zq_ref[...] = z[:, :ZQ_W]
    zs_ref[...] = z[:, ZQ_W:]


def _in_proj(x, mod5, g_mix, w_in, layer, row_fn, tm):
    t, d = x.shape
    n = w_in.shape[-1]
    mspec = lambda chunk: pl.BlockSpec((None, None, None, 1, d),
                                       lambda i: (layer, row_fn(i), chunk, 0, 0))
    return pl.pallas_call(
        _in_proj_kernel,
        grid=(t // tm,),
        in_specs=[
            pl.BlockSpec((tm, d), lambda i: (i, 0)),
            pl.BlockSpec((None, 1, d), lambda i: (layer, 0, 0)),
            mspec(0), mspec(1),
            _resident((None, d, n), lambda i: (layer, 0, 0)),
        ],
        out_specs=[pl.BlockSpec((tm, ZQ_W), lambda i: (i, 0)),
                   pl.BlockSpec((tm, ZS_W), lambda i: (i, 0))],
        out_shape=[jax.ShapeDtypeStruct((t, ZQ_W), F32), jax.ShapeDtypeStruct((t, ZS_W), F32)],
        compiler_params=_cparams(("parallel",)),
        name="in_proj",
    )(x, g_mix.reshape(-1, 1, d), mod5, mod5, w_in)


def _mixer_kernel(*refs, seq, past, qb, latent):
    it = iter(refs)
    zq_ref, zs_ref, x_ref = next(it), next(it), next(it)
    gt1_ref, sh2_ref, sc2_ref = next(it), next(it), next(it)
    gq_ref, gk_ref, gmq_ref, gmkv_ref = next(it), next(it), next(it), next(it)
    convw_ref, ggrp_ref, gffn_ref = next(it), next(it), next(it)
    wuq_ref, wukv_ref, wout_ref, wrt_ref = next(it), next(it), next(it), next(it)
    if latent:
        ck_ref, cv_ref, cckv_ref, ckpe_ref = next(it), next(it), next(it), next(it)
        cosa_ref, sina_ref, cosc_ref, sinc_ref = next(it), next(it), next(it), next(it)
    xo_ref, h2_ref, lg_ref = next(it), next(it), next(it)
    if not latent:
        nk_ref, nv_ref, nckv_ref, nkpe_ref = next(it), next(it), next(it), next(it)
    kbf_ref, vbf_ref, kvm_ref, kpe_ref, conv_ref, u_ref = (next(it) for _ in range(6))

    j = pl.program_id(1)

    @pl.when(j == 0)
    def _per_sequence():
        if latent:
            kbf_ref[:past, :] = ck_ref[...].astype(BF16)
            vbf_ref[:past, :] = cv_ref[...].astype(BF16)
            kpe_ref[:past, :MLA_ROPE] = ckpe_ref[...].astype(BF16)
            kpe_ref[:past, MLA_ROPE:] = jnp.zeros((past, LANES - MLA_ROPE), BF16)
            for c0 in range(0, past, qb):
                c1 = min(c0 + qb, past)
                kvm_ref[c0:c1, :] = _dot(cckv_ref[c0:c1, :].astype(BF16), wukv_ref[...]).astype(BF16)
        u_ref[:CONV_HALO, :] = jnp.zeros((CONV_HALO, CONV_CH), F32)
        u_ref[CONV_HALO + seq:, :] = jnp.zeros((CONV_HALO, CONV_CH), F32)
        for c0 in range(0, seq, qb):
            rows, prow = slice(c0, c0 + qb), slice(past + c0, past + c0 + qb)
            for hk in range(GQA_KV_HEADS):
                sl = slice(hk * HEAD_DIM, (hk + 1) * HEAD_DIM)
                k = _rms(zs_ref[rows, ZS_K + hk * HEAD_DIM:ZS_K + (hk + 1) * HEAD_DIM], gk_ref[...])
                if latent:
                    k = _rope(k, cosa_ref[rows, :], sina_ref[rows, :], HEAD_DIM // 4)
                else:
                    nk_ref[rows, sl] = k
                kbf_ref[prow, sl] = k.astype(BF16)
            v = zs_ref[rows, ZS_V:ZS_V + KV_W]
            vbf_ref[prow, :] = v.astype(BF16)
            ckv_n = _rms(zs_ref[rows, ZS_CKV:ZS_CKV + MLA_KV_RANK], gmkv_ref[...])
            kpe = zs_ref[rows, ZS_KR:ZS_KR + LANES]
            if latent:
                kpe = _rope(kpe, cosc_ref[rows, :], sinc_ref[rows, :], MLA_ROPE // 4)
            else:
                nv_ref[rows, :] = v
                nckv_ref[rows, :] = ckv_n
                nkpe_ref[rows, :] = kpe[:, :MLA_ROPE]
            kpe_ref[prow, :] = kpe.astype(BF16)
            kvm_ref[prow, :] = _dot(ckv_n.astype(BF16), wukv_ref[...]).astype(BF16)
            u_ref[CONV_HALO + c0:CONV_HALO + c0 + qb, :] = (
                zs_ref[rows, ZS_CC:ZS_CC + CONV_CH] * zs_ref[rows, ZS_CH:ZS_CH + CONV_CH])
        for c0 in range(0, seq, qb):
            taps = [u_ref[CONV_HALO - 1 + c0 + i:CONV_HALO - 1 + c0 + i + qb, :] * convw_ref[i:i + 1, :]
                    for i in range(3)]
            conv_ref[c0:c0 + qb, :] = taps[0] + taps[1] + taps[2]

    r0 = pl.multiple_of(j * qb, qb)

    def attend(s, v_bf):
        m = jnp.max(s, axis=-1, keepdims=True)
        e = jnp.exp(s - m)
        return _dot(e.astype(BF16), v_bf) / jnp.sum(e, axis=-1, keepdims=True)

    outs_a = []
    for h in range(GQA_HEADS):
        hk = h // (GQA_HEADS // GQA_KV_HEADS)
        q = _rms(zq_ref[:, h * HEAD_DIM:(h + 1) * HEAD_DIM], gq_ref[...])
        if latent:
            q = _rope(q, cosa_ref[pl.ds(r0, qb), :], sina_ref[pl.ds(r0, qb), :], HEAD_DIM // 4)
        ksl = slice(hk * HEAD_DIM, (hk + 1) * HEAD_DIM)
        s = _dot_nt(q.astype(BF16), kbf_ref[:, ksl]) * (HEAD_DIM ** -0.5)
        outs_a.append(attend(s, vbf_ref[:, ksl]))
    out_a = jnp.concatenate(outs_a, axis=-1)

    cq_n = _rms(zq_ref[:, GQA_WIDTH:GQA_WIDTH + MLA_Q_RANK], gmq_ref[...])
    q_m = _dot(cq_n.astype(BF16), wuq_ref[...])
    outs_c = []
    for h in range(MLA_HEADS):
        q_nope = q_m[:, h * MLA_QH:h * MLA_QH + MLA_NOPE]
        q_pe = q_m[:, h * MLA_QH + MLA_NOPE:(h + 1) * MLA_QH]
        if latent:
            q_pe = _rope(q_pe, cosc_ref[pl.ds(r0, qb), :], sinc_ref[pl.ds(r0, qb), :], MLA_ROPE // 4)
        s = (_dot_nt(q_nope.astype(BF16), kvm_ref[:, h * MLA_KVH:h * MLA_KVH + MLA_NOPE])
             + _dot_nt(q_pe.astype(BF16), kpe_ref[...])) * ((MLA_NOPE + MLA_ROPE) ** -0.5)
        outs_c.append(attend(s, kvm_ref[:, h * MLA_KVH + MLA_NOPE:(h + 1) * MLA_KVH]))
    out_c = jnp.concatenate(outs_c, axis=-1)

    out_b = zq_ref[:, GQA_WIDTH + MLA_Q_RANK:] * conv_ref[pl.ds(r0, qb), :]

    g = ggrp_ref
    merged = jnp.concatenate([
        _rms(out_a, g[:, :GQA_WIDTH]).astype(BF16),
        _rms(out_b, g[:, GQA_WIDTH:GQA_WIDTH + CONV_CH]).astype(BF16),
        _rms(out_c, g[:, GQA_WIDTH + CONV_CH:]).astype(BF16)], axis=-1)
    x_new = x_ref[...] + gt1_ref[...] * _dot(merged, wout_ref[...])
    xo_ref[...] = x_new
    h2 = _rms(x_new, gffn_ref[...]) * (1.0 + sc2_ref[...]) + sh2_ref[...]
    h2_ref[...] = h2.astype(BF16)
    lg_ref[...] = lax.dot_general(wrt_ref[...], h2, (((1,), (1,)), ((), ())),
                                  precision=lax.Precision.HIGHEST, preferred_element_type=F32)


def _mixer(zq, zs, x, mod5, p, layer, *, n_seq, seq, row0, mod_row_fn, cache=None, rope=None, qb=256):
    d = x.shape[1]
    t_out = n_seq * seq
    latent = cache is not None
    past = cache[0].shape[2] if latent else 0
    nq = seq // qb
    qblk0, sblk0 = row0 // qb, row0 // seq
    sk = past + seq

    def const(shape):
        return _resident(shape, lambda b, j: (0,) * len(shape))

    def lyr(shape):
        return _resident((None,) + shape, lambda b, j: (layer,) + (0,) * len(shape))

    mspec = lambda chunk: pl.BlockSpec((None, None, None, 1, d),
                                       lambda b, j: (layer, mod_row_fn(b), chunk, 0, 0))
    qrow = lambda b, j: (qblk0 + b * nq + j, 0)
    orow = lambda b, j: (b * nq + j, 0)
    seq_spec = _resident if nq > 1 else pl.BlockSpec
    in_specs = [
        pl.BlockSpec((qb, ZQ_W), qrow),
        seq_spec((seq, ZS_W), lambda b, j: (sblk0 + b, 0)),
        pl.BlockSpec((qb, d), qrow),
        mspec(2), mspec(3), mspec(4),
        lyr((1, HEAD_DIM)), lyr((1, HEAD_DIM)), lyr((1, MLA_Q_RANK)), lyr((1, MLA_KV_RANK)),
        lyr((3, CONV_CH)), lyr((1, d)), lyr((1, d)),
        lyr((MLA_Q_RANK, MLA_HEADS * MLA_QH)), lyr((MLA_KV_RANK, MLA_HEADS * MLA_KVH)),
        lyr((d, d)), lyr((N_EXPERTS, d)),
    ]
    args = [zq, zs, x, mod5, mod5, mod5,
            p["g_q"], p["g_k"], p["g_mla_q"], p["g_mla_kv"], p["conv_w"], p["g_grp"], p["g_ffn"],
            p["w_uq"], p["w_ukv"], p["w_out"], p["w_router_t"]]
    if latent:
        cspec = lambda w: pl.BlockSpec((None, None, past, w), lambda b, j: (b, layer, 0, 0))
        in_specs += [cspec(KV_W), cspec(KV_W), cspec(MLA_KV_RANK), cspec(MLA_ROPE)]
        in_specs += [const((seq, LANES))] * 4
        args += list(cache) + list(rope)
    out_specs = [pl.BlockSpec((qb, d), orow), pl.BlockSpec((qb, d), orow),
                 pl.BlockSpec((N_EXPERTS, qb), lambda b, j: (0, b * nq + j))]
    out_shape = [jax.ShapeDtypeStruct((t_out, d), F32), jax.ShapeDtypeStruct((t_out, d), BF16),
                 jax.ShapeDtypeStruct((N_EXPERTS, t_out), F32)]
    if not latent:
        sspec = lambda w: pl.BlockSpec((None, seq, w), lambda b, j: (b, 0, 0))
        out_specs += [sspec(KV_W), sspec(KV_W), sspec(MLA_KV_RANK), sspec(MLA_ROPE)]
        out_shape += [jax.ShapeDtypeStruct((n_seq, seq, w), F32)
                      for w in (KV_W, KV_W, MLA_KV_RANK, MLA_ROPE)]
    scratch = [pltpu.VMEM((sk, KV_W), BF16), pltpu.VMEM((sk, KV_W), BF16),
               pltpu.VMEM((sk, MLA_HEADS * MLA_KVH), BF16), pltpu.VMEM((sk, LANES), BF16),
               pltpu.VMEM((seq, CONV_CH), F32), pltpu.VMEM((seq + 2 * CONV_HALO, CONV_CH), F32)]
    return pl.pallas_call(
        functools.partial(_mixer_kernel, seq=seq, past=past, qb=qb, latent=latent),
        grid=(n_seq, nq),
        in_specs=in_specs, out_specs=out_specs, out_shape=out_shape, scratch_shapes=scratch,
        compiler_params=_cparams(("parallel", "arbitrary")),
        name="mixer_latent" if latent else "mixer_context",
    )(*args)


def _router_kernel(lg_ref, b_ref, comb_ref):
    tc = lg_ref.shape[1]
    shape3 = (N_GROUPS, GROUP_SIZE, tc)
    scores = jax.nn.sigmoid(lg_ref[...])
    choice = (scores + b_ref[...]).reshape(shape3)
    scores = scores.reshape(shape3)
    neg = -jnp.inf
    member = lax.broadcasted_iota(jnp.int32, shape3, 1)
    m1 = jnp.max(choice, axis=1, keepdims=True)
    first = jnp.min(jnp.where(choice == m1, member, GROUP_SIZE), axis=1, keepdims=True)
    m2 = jnp.max(jnp.where(member == first, neg, choice), axis=1, keepdims=True)
    gscore = m1 + m2
    gid = lax.broadcasted_iota(jnp.int32, gscore.shape, 0)
    gmask = jnp.zeros(gscore.shape, jnp.int32)
    for _ in range(TOPK_GROUPS):
        m = jnp.max(gscore, axis=0, keepdims=True)
        pick = jnp.min(jnp.where(gscore == m, gid, N_GROUPS), axis=0, keepdims=True)
        hit = gid == pick
        gmask = jnp.where(hit, 1, gmask)
        gscore = jnp.where(hit, neg, gscore)
    eid = lax.broadcasted_iota(jnp.int32, shape3, 0) * GROUP_SIZE + member
    cur = jnp.where(gmask > 0, choice, neg)
    sel = jnp.zeros(shape3, jnp.int32)
    for _ in range(TOP_K):
        m = jnp.max(jnp.max(cur, axis=0, keepdims=True), axis=1, keepdims=True)
        cand = jnp.where(cur == m, eid, N_EXPERTS)
        pick = jnp.min(jnp.min(cand, axis=0, keepdims=True), axis=1, keepdims=True)
        hit = eid == pick
        sel = jnp.where(hit, 1, sel)
        cur = jnp.where(hit, neg, cur)
    w = jnp.where(sel > 0, scores, 0.0)
    wsum = jnp.sum(jnp.sum(w, axis=0, keepdims=True), axis=1, keepdims=True)
    comb = (w / wsum * ROUTED_SCALE).reshape(N_EXPERTS, tc)
    comb = jnp.concatenate([comb, jnp.zeros((LANES - N_EXPERTS, tc), F32)], axis=0)
    comb_ref[...] = comb.T


def _router(logits_t, b_router, layer, tc=512):
    e, t = logits_t.shape
    return pl.pallas_call(
        _router_kernel,
        grid=(t // tc,),
        in_specs=[pl.BlockSpec((e, tc), lambda i: (0, i)),
                  pl.BlockSpec((None, e, 1), lambda i: (layer, 0, 0))],
        out_specs=pl.BlockSpec((tc, LANES), lambda i: (i, 0)),
        out_shape=jax.ShapeDtypeStruct((t, LANES), F32),
        compiler_params=_cparams(("parallel",)),
        name="router",
    )(logits_t, b_router.reshape(-1, e, 1))


def _moe_kernel(h_ref, comb_ref, x_ref, gt2_ref, wg_ref, wu_ref, wd_ref, sg_ref, su_ref, sd_ref,
                o_ref, acc_ref):
    e = pl.program_id(1)
    h = h_ref[...]

    @pl.when(e == 0)
    def _():
        acc_ref[...] = jnp.zeros_like(acc_ref)

    f = wg_ref.shape[-1]
    onehot = (lax.broadcasted_iota(jnp.int32, (LANES, f), 0) == e).astype(F32)
    c = jnp.dot(comb_ref[...], onehot, precision=lax.Precision.HIGHEST, preferred_element_type=F32)
    hg = _dot(h, wg_ref[...])
    a = hg * jax.nn.sigmoid(hg) * _dot(h, wu_ref[...]) * c
    acc_ref[...] += _dot(a.astype(BF16), wd_ref[...])

    @pl.when(e == pl.num_programs(1) - 1)
    def _():
        sgt = _dot(h, sg_ref[...])
        shared = _dot((sgt * jax.nn.sigmoid(sgt) * _dot(h, su_ref[...])).astype(BF16), sd_ref[...])
        o_ref[...] = x_ref[...] + gt2_ref[...] * (acc_ref[...] + shared)


def _moe(h2, comb, x, mod5, p, layer, row_fn, tm=512):
    t, d = x.shape
    f = p["w_gate"].shape[-1]
    tok = lambda i, e: (i, 0)
    lyr = lambda shape: pl.BlockSpec((None,) + shape, lambda i, e: (layer,) + (0,) * len(shape))
    return pl.pallas_call(
        _moe_kernel,
        grid=(t // tm, N_EXPERTS),
        in_specs=[
            pl.BlockSpec((tm, d), tok), pl.BlockSpec((tm, LANES), tok), pl.BlockSpec((tm, d), tok),
            pl.BlockSpec((None, None, None, 1, d), lambda i, e: (layer, row_fn(i), 5, 0, 0)),
            pl.BlockSpec((None, None, d, f), lambda i, e: (layer, e, 0, 0)),
            pl.BlockSpec((None, None, d, f), lambda i, e: (layer, e, 0, 0)),
            pl.BlockSpec((None, None, f, d), lambda i, e: (layer, e, 0, 0)),
            lyr((d, f)), lyr((d, f)), lyr((f, d)),
        ],
        out_specs=pl.BlockSpec((tm, d), tok),
        out_shape=jax.ShapeDtypeStruct((t, d), F32),
        scratch_shapes=[pltpu.VMEM((tm, d), F32)],
        compiler_params=_cparams(("parallel", "arbitrary")),
        name="moe_dense",
    )(h2, comb, x, mod5, p["w_gate"], p["w_up"], p["w_down"], p["ws_gate"], p["ws_up"], p["ws_down"])


def _final_norm_kernel(x_ref, g_ref, o_ref):
    o_ref[...] = _rms(x_ref[...], g_ref[...])


def _final_norm(x, g, tm=512):
    t, d = x.shape
    return pl.pallas_call(
        _final_norm_kernel,
        grid=(t // tm,),
        in_specs=[pl.BlockSpec((tm, d), lambda i: (i, 0)), pl.BlockSpec((1, d), lambda i: (0, 0))],
        out_specs=pl.BlockSpec((tm, d), lambda i: (i, 0)),
        out_shape=jax.ShapeDtypeStruct((t, d), F32),
        compiler_params=_cparams(("parallel",)),
        name="final_norm",
    )(x, g.reshape(1, d))


def _rope_tables(n_tokens, dim):
    rows = n_tokens // GRID_W
    row = jnp.repeat(jnp.arange(rows, dtype=jnp.int32), GRID_W).astype(F32)
    col = jnp.tile(jnp.arange(GRID_W, dtype=jnp.int32), rows).astype(F32)
    half = dim // 2
    inv_freq = ROPE_THETA ** (-(jnp.arange(half // 2, dtype=F32) * 2.0 / half))
    ang_r = row[:, None] * inv_freq[None, :]
    ang_c = col[:, None] * inv_freq[None, :]
    cos = jnp.concatenate([jnp.cos(ang_r)] * 2 + [jnp.cos(ang_c)] * 2, axis=-1)
    sin = jnp.concatenate([-jnp.sin(ang_r), jnp.sin(ang_r), -jnp.sin(ang_c), jnp.sin(ang_c)], axis=-1)
    pad = ((0, 0), (0, LANES - dim))
    return jnp.pad(cos, pad), jnp.pad(sin, pad)


def _prep_params(w_in, w_uq, w_ukv, w_out, w_router, w_gate, w_up, w_down, ws_gate, ws_up, ws_down):
    depth = w_in.shape[0]
    q, k, v, cb, cc, ch, cq, ckv, kr = jnp.split(
        w_in, [768, 1024, 1280, 1792, 2304, 2816, 3328, 3584], axis=-1)
    kr = jnp.pad(kr, ((0, 0), (0, 0), (0, LANES - MLA_ROPE)))
    w_in_r = jnp.concatenate([q, cq, cb, k, v, ckv, cc, ch, kr], axis=-1).astype(BF16)
    w_uq_r = jnp.pad(w_uq.reshape(depth, MLA_Q_RANK, MLA_HEADS, MLA_NOPE + MLA_ROPE),
                     ((0, 0), (0, 0), (0, 0), (0, MLA_QH - MLA_NOPE - MLA_ROPE)))
    w_uq_r = w_uq_r.reshape(depth, MLA_Q_RANK, MLA_HEADS * MLA_QH).astype(BF16)
    return {
        "w_in": w_in_r, "w_uq": w_uq_r, "w_ukv": w_ukv.astype(BF16), "w_out": w_out.astype(BF16),
        "w_router_t": jnp.swapaxes(w_router, 1, 2),
        "w_gate": w_gate.astype(BF16), "w_up": w_up.astype(BF16), "w_down": w_down.astype(BF16),
        "ws_gate": ws_gate.astype(BF16), "ws_up": ws_up.astype(BF16), "ws_down": ws_down.astype(BF16),
    }


def kernel(x_prompt, x_sample, c, cache_gqa_k, cache_gqa_v, cache_mla_ckv, cache_mla_kpe, c_ctx, w_mod, b_mod, g_mix, w_in, g_q, g_k, conv_w, g_mla_q, g_mla_kv, w_uq, w_ukv, g_grp, w_out, g_ffn, w_router, b_router, w_gate, w_up, w_down, ws_gate, ws_up, ws_down, g_final):
    batch, seq, d = x_prompt.shape
    dec_batch, dec_seq, _ = x_sample.shape
    depth = w_mod.shape[0]
    past = cache_gqa_k.shape[2]
    t_ctx, t_lat = batch * seq, dec_batch * dec_seq
    tm = 256
    assert seq % tm == 0 and dec_seq % tm == 0 and t_ctx % dec_seq == 0

    p = _prep_params(w_in, w_uq, w_ukv, w_out, w_router, w_gate, w_up, w_down, ws_gate, ws_up, ws_down)
    for name, val in (("g_q", g_q), ("g_k", g_k), ("g_mla_q", g_mla_q), ("g_mla_kv", g_mla_kv),
                      ("g_grp", g_grp), ("g_ffn", g_ffn)):
        p[name] = val.reshape(depth, 1, -1)
    p["conv_w"] = conv_w

    mod_rows = 8
    cond = jnp.zeros((mod_rows, d), F32).at[0].set(c_ctx).at[1:1 + dec_batch].set(c)
    mod5 = _modulation(cond, w_mod, b_mod).reshape(depth, mod_rows, 6, 1, d)

    def tile_row_fn(tile):
        n_ctx = t_ctx // tile
        per_seq = dec_seq // tile
        return lambda i: jnp.where(i < n_ctx, 0, 1 + (i - n_ctx) // per_seq)

    cache = (cache_gqa_k.reshape(dec_batch, depth, past, KV_W),
             cache_gqa_v.reshape(dec_batch, depth, past, KV_W), cache_mla_ckv, cache_mla_kpe)
    rope = _rope_tables(dec_seq, HEAD_DIM) + _rope_tables(dec_seq, MLA_ROPE)

    x = jnp.concatenate([x_prompt.reshape(t_ctx, d), x_sample.reshape(t_lat, d)], axis=0)
    new_ctx = []
    for layer in range(depth):
        zq, zs = _in_proj(x, mod5, g_mix, p["w_in"], layer, tile_row_fn(tm), tm)
        xc, h2c, lgc, nk, nv, nckv, nkpe = _mixer(
            zq, zs, x, mod5, p, layer, n_seq=batch, seq=seq, row0=0, mod_row_fn=lambda b: 0)
        xl, h2l, lgl = _mixer(
            zq, zs, x, mod5, p, layer, n_seq=dec_batch, seq=dec_seq, row0=t_ctx,
            mod_row_fn=lambda b: 1 + b, cache=cache, rope=rope, qb=128)
        new_ctx.append((nk, nv, nckv, nkpe))
        x_mid = jnp.concatenate([xc, xl], axis=0)
        h2 = jnp.concatenate([h2c, h2l], axis=0)
        logits_t = jnp.concatenate([lgc, lgl], axis=1)
        comb = _router(logits_t, b_router, layer)
        x = _moe(h2, comb, x_mid, mod5, p, layer, tile_row_fn(512))
    y = _final_norm(x, g_final)
    y_prompt = y[:t_ctx].reshape(batch, seq, d)
    y_sample = y[t_ctx:].reshape(dec_batch, dec_seq, d)
    stack = lambda i: jnp.stack([lc[i] for lc in new_ctx], axis=1)
    new_k = stack(0).reshape(batch, depth, seq, GQA_KV_HEADS, HEAD_DIM)
    new_v = stack(1).reshape(batch, depth, seq, GQA_KV_HEADS, HEAD_DIM)
    return (y_prompt, y_sample, new_k, new_v, stack(2), stack(3))
```

```python
import functools

import jax
import jax.numpy as jnp
from jax import lax
from jax.experimental import pallas as pl
from jax.experimental.pallas import tpu as pltpu

F32 = jnp.float32
BF16 = jnp.bfloat16
I32 = jnp.int32

EPS = 1e-6
ROPE_THETA = 10000.0
GRID_W = 64
GQA_HEADS = 6
GQA_KV_HEADS = 2
HEAD_DIM = 128
CONV_CH = 512
MLA_HEADS = 6
MLA_Q_RANK = 512
MLA_KV_RANK = 256
MLA_NOPE = 128
MLA_ROPE = 64
MLA_V = 128
GQA_WIDTH = GQA_HEADS * HEAD_DIM
MLA_WIDTH = MLA_HEADS * MLA_V
N_EXPERTS = 64
TOP_K = 8
N_GROUPS = 8
TOPK_GROUPS = 4
GROUP_SIZE = N_EXPERTS // N_GROUPS
ROUTED_SCALE = 2.5
MOE_TM = 256

LANES = 128
VMEM_LIMIT_BYTES = 56 * 1024 * 1024

ZQ_W = GQA_WIDTH + MLA_Q_RANK + CONV_CH
KV_W = GQA_KV_HEADS * HEAD_DIM
ZS_W = 2 * KV_W + MLA_KV_RANK + 2 * CONV_CH + LANES
ZS_K, ZS_V, ZS_CKV = 0, KV_W, 2 * KV_W
ZS_CC = ZS_CKV + MLA_KV_RANK
ZS_CH = ZS_CC + CONV_CH
ZS_KR = ZS_CH + CONV_CH
CONV_HALO = 8
MLA_QH = 2 * LANES
MLA_KVH = MLA_NOPE + MLA_V


def _cparams(sem, vmem=VMEM_LIMIT_BYTES):
    return pltpu.CompilerParams(dimension_semantics=sem, vmem_limit_bytes=vmem)


def _resident(shape, index_map):
    return pl.BlockSpec(shape, index_map, pipeline_mode=pl.Buffered(1))


def _rms(x, g):
    ms = jnp.mean(x * x, axis=-1, keepdims=True)
    return x * lax.rsqrt(ms + EPS) * g


def _dot(a, b):
    return jnp.dot(a, b, preferred_element_type=F32)


def _dot_nt(a, b):
    return lax.dot_general(a, b, (((1,), (1,)), ((), ())), preferred_element_type=F32)


def _silu(x):
    return x * jax.nn.sigmoid(x)


def _rope(x, cos, sin_signed, hb):
    lane = lax.broadcasted_iota(jnp.int32, x.shape, 1)
    partner = jnp.where((lane % (2 * hb)) < hb,
                        pltpu.roll(x, LANES - hb, axis=1), pltpu.roll(x, hb, axis=1))
    return x * cos + partner * sin_signed


def _mod_kernel(c_ref, w_ref, b_ref, o_ref):
    c = c_ref[...]
    a = (c * jax.nn.sigmoid(c)).astype(BF16)
    o_ref[...] = _dot(a, w_ref[...].astype(BF16)) + b_ref[...]


def _modulation(cond, w_mod, b_mod, tn=1024):
    depth, d, n = w_mod.shape
    rows = cond.shape[0]
    return pl.pallas_call(
        _mod_kernel,
        grid=(depth, n // tn),
        in_specs=[
            pl.BlockSpec((rows, d), lambda l, j: (0, 0)),
            pl.BlockSpec((None, d, tn), lambda l, j: (l, 0, j)),
            pl.BlockSpec((None, 1, tn), lambda l, j: (l, 0, j)),
        ],
        out_specs=pl.BlockSpec((None, rows, tn), lambda l, j: (l, 0, j)),
        out_shape=jax.ShapeDtypeStruct((depth, rows, n), F32),
        compiler_params=_cparams(("parallel", "parallel")),
        name="modulation",
    )(cond, w_mod, b_mod.reshape(depth, 1, n))


def _in_proj_kernel(x_ref, g_ref, sh_ref, sc_ref, w_ref, zq_ref, zs_ref):
    h = _rms(x_ref[...], g_ref[...]) * (1.0 + sc_ref[...]) + sh_ref[...]
    z = _dot(h.astype(BF16), w_ref[...])
    zq_ref[...] = z[:, :ZQ_W]
    zs_ref[...] = z[:, ZQ_W:]


def _in_proj(x, mod5, g_mix, w_in, layer, row_fn, tm):
    t, d = x.shape
    n = w_in.shape[-1]
    mspec = lambda chunk: pl.BlockSpec((None, None, None, 1, d),
                                       lambda i: (layer, row_fn(i), chunk, 0, 0))
    return pl.pallas_call(
        _in_proj_kernel,
        grid=(t // tm,),
        in_specs=[
            pl.BlockSpec((tm, d), lambda i: (i, 0)),
            pl.BlockSpec((None, 1, d), lambda i: (layer, 0, 0)),
            mspec(0), mspec(1),
            _resident((None, d, n), lambda i: (layer, 0, 0)),
        ],
        out_specs=[pl.BlockSpec((tm, ZQ_W), lambda i: (i, 0)),
                   pl.BlockSpec((tm, ZS_W), lambda i: (i, 0))],
        out_shape=[jax.ShapeDtypeStruct((t, ZQ_W), F32), jax.ShapeDtypeStruct((t, ZS_W), F32)],
        compiler_params=_cparams(("parallel",)),
        name="in_proj",
    )(x, g_mix.reshape(-1, 1, d), mod5, mod5, w_in)


def _mixer_kernel(*refs, seq, past, qb, latent):
    it = iter(refs)
    zq_ref, zs_ref, x_ref = next(it), next(it), next(it)
    gt1_ref, sh2_ref, sc2_ref = next(it), next(it), next(it)
    gq_ref, gk_ref, gmq_ref, gmkv_ref = next(it), next(it), next(it), next(it)
    convw_ref, ggrp_ref, gffn_ref = next(it), next(it), next(it)
    wuq_ref, wukv_ref, wout_ref, wrt_ref = next(it), next(it), next(it), next(it)
    if latent:
        ck_ref, cv_ref, cckv_ref, ckpe_ref = next(it), next(it), next(it), next(it)
        cosa_ref, sina_ref, cosc_ref, sinc_ref = next(it), next(it), next(it), next(it)
    xo_ref, h2_ref, lg_ref = next(it), next(it), next(it)
    if not latent:
        nk_ref, nv_ref, nckv_ref, nkpe_ref = next(it), next(it), next(it), next(it)
    kbf_ref, vbf_ref, kvm_ref, kpe_ref, conv_ref, u_ref = (next(it) for _ in range(6))

    j = pl.program_id(1)

    @pl.when(j == 0)
    def _per_sequence():
        if latent:
            kbf_ref[:past, :] = ck_ref[...].astype(BF16)
            vbf_ref[:past, :] = cv_ref[...].astype(BF16)
            kpe_ref[:past, :MLA_ROPE] = ckpe_ref[...].astype(BF16)
            kpe_ref[:past, MLA_ROPE:] = jnp.zeros((past, LANES - MLA_ROPE), BF16)
            for c0 in range(0, past, qb):
                c1 = min(c0 + qb, past)
                kvm_ref[c0:c1, :] = _dot(cckv_ref[c0:c1, :].astype(BF16), wukv_ref[...]).astype(BF16)
        u_ref[:CONV_HALO, :] = jnp.zeros((CONV_HALO, CONV_CH), F32)
        u_ref[CONV_HALO + seq:, :] = jnp.zeros((CONV_HALO, CONV_CH), F32)
        for c0 in range(0, seq, qb):
            rows, prow = slice(c0, c0 + qb), slice(past + c0, past + c0 + qb)
            for hk in range(GQA_KV_HEADS):
                sl = slice(hk * HEAD_DIM, (hk + 1) * HEAD_DIM)
                k = _rms(zs_ref[rows, ZS_K + hk * HEAD_DIM:ZS_K + (hk + 1) * HEAD_DIM], gk_ref[...])
                if latent:
                    k = _rope(k, cosa_ref[rows, :], sina_ref[rows, :], HEAD_DIM // 4)
                else:
                    nk_ref[rows, sl] = k
                kbf_ref[prow, sl] = k.astype(BF16)
            v = zs_ref[rows, ZS_V:ZS_V + KV_W]
            vbf_ref[prow, :] = v.astype(BF16)
            ckv_n = _rms(zs_ref[rows, ZS_CKV:ZS_CKV + MLA_KV_RANK], gmkv_ref[...])
            kpe = zs_ref[rows, ZS_KR:ZS_KR + LANES]
            if latent:
                kpe = _rope(kpe, cosc_ref[rows, :], sinc_ref[rows, :], MLA_ROPE // 4)
            else:
                nv_ref[rows, :] = v
                nckv_ref[rows, :] = ckv_n
                nkpe_ref[rows, :] = kpe[:, :MLA_ROPE]
            kpe_ref[prow, :] = kpe.astype(BF16)
            kvm_ref[prow, :] = _dot(ckv_n.astype(BF16), wukv_ref[...]).astype(BF16)
            u_ref[CONV_HALO + c0:CONV_HALO + c0 + qb, :] = (
                zs_ref[rows, ZS_CC:ZS_CC + CONV_CH] * zs_ref[rows, ZS_CH:ZS_CH + CONV_CH])
        for c0 in range(0, seq, qb):
            taps = [u_ref[CONV_HALO - 1 + c0 + i:CONV_HALO - 1 + c0 + i + qb, :] * convw_ref[i:i + 1, :]
                    for i in range(3)]
            conv_ref[c0:c0 + qb, :] = taps[0] + taps[1] + taps[2]

    r0 = pl.multiple_of(j * qb, qb)

    def attend(s, v_bf):
        m = jnp.max(s, axis=-1, keepdims=True)
        e = jnp.exp(s - m)
        return _dot(e.astype(BF16), v_bf) / jnp.sum(e, axis=-1, keepdims=True)

    outs_a = []
    for h in range(GQA_HEADS):
        hk = h // (GQA_HEADS // GQA_KV_HEADS)
        q = _rms(zq_ref[:, h * HEAD_DIM:(h + 1) * HEAD_DIM], gq_ref[...])
        if latent:
            q = _rope(q, cosa_ref[pl.ds(r0, qb), :], sina_ref[pl.ds(r0, qb), :], HEAD_DIM // 4)
        ksl = slice(hk * HEAD_DIM, (hk + 1) * HEAD_DIM)
        s = _dot_nt(q.astype(BF16), kbf_ref[:, ksl]) * (HEAD_DIM ** -0.5)
        outs_a.append(attend(s, vbf_ref[:, ksl]))
    out_a = jnp.concatenate(outs_a, axis=-1)

    cq_n = _rms(zq_ref[:, GQA_WIDTH:GQA_WIDTH + MLA_Q_RANK], gmq_ref[...])
    q_m = _dot(cq_n.astype(BF16), wuq_ref[...])
    outs_c = []
    for h in range(MLA_HEADS):
        q_nope = q_m[:, h * MLA_QH:h * MLA_QH + MLA_NOPE]
        q_pe = q_m[:, h * MLA_QH + MLA_NOPE:(h + 1) * MLA_QH]
        if latent:
            q_pe = _rope(q_pe, cosc_ref[pl.ds(r0, qb), :], sinc_ref[pl.ds(r0, qb), :], MLA_ROPE // 4)
        s = (_dot_nt(q_nope.astype(BF16), kvm_ref[:, h * MLA_KVH:h * MLA_KVH + MLA_NOPE])
             + _dot_nt(q_pe.astype(BF16), kpe_ref[...])) * ((MLA_NOPE + MLA_ROPE) ** -0.5)
        outs_c.append(attend(s, kvm_ref[:, h * MLA_KVH + MLA_NOPE:(h + 1) * MLA_KVH]))
    out_c = jnp.concatenate(outs_c, axis=-1)

    out_b = zq_ref[:, GQA_WIDTH + MLA_Q_RANK:] * conv_ref[pl.ds(r0, qb), :]

    g = ggrp_ref
    merged = jnp.concatenate([
        _rms(out_a, g[:, :GQA_WIDTH]).astype(BF16),
        _rms(out_b, g[:, GQA_WIDTH:GQA_WIDTH + CONV_CH]).astype(BF16),
        _rms(out_c, g[:, GQA_WIDTH + CONV_CH:]).astype(BF16)], axis=-1)
    x_new = x_ref[...] + gt1_ref[...] * _dot(merged, wout_ref[...])
    xo_ref[...] = x_new
    h2 = _rms(x_new, gffn_ref[...]) * (1.0 + sc2_ref[...]) + sh2_ref[...]
    h2_ref[...] = h2
    lg_ref[...] = lax.dot_general(wrt_ref[...], h2, (((1,), (1,)), ((), ())),
                                  precision=lax.Precision.HIGHEST, preferred_element_type=F32)


def _mixer(zq, zs, x, mod5, p, layer, *, n_seq, seq, row0, mod_row_fn, cache=None, rope=None, qb=256):
    d = x.shape[1]
    t_out = n_seq * seq
    latent = cache is not None
    past = cache[0].shape[2] if latent else 0
    nq = seq // qb
    qblk0, sblk0 = row0 // qb, row0 // seq
    sk = past + seq

    def const(shape):
        return _resident(shape, lambda b, j: (0,) * len(shape))

    def lyr(shape):
        return _resident((None,) + shape, lambda b, j: (layer,) + (0,) * len(shape))

    mspec = lambda chunk: pl.BlockSpec((None, None, None, 1, d),
                                       lambda b, j: (layer, mod_row_fn(b), chunk, 0, 0))
    qrow = lambda b, j: (qblk0 + b * nq + j, 0)
    orow = lambda b, j: (b * nq + j, 0)
    seq_spec = _resident if nq > 1 else pl.BlockSpec
    in_specs = [
        pl.BlockSpec((qb, ZQ_W), qrow),
        seq_spec((seq, ZS_W), lambda b, j: (sblk0 + b, 0)),
        pl.BlockSpec((qb, d), qrow),
        mspec(2), mspec(3), mspec(4),
        lyr((1, HEAD_DIM)), lyr((1, HEAD_DIM)), lyr((1, MLA_Q_RANK)), lyr((1, MLA_KV_RANK)),
        lyr((3, CONV_CH)), lyr((1, d)), lyr((1, d)),
        lyr((MLA_Q_RANK, MLA_HEADS * MLA_QH)), lyr((MLA_KV_RANK, MLA_HEADS * MLA_KVH)),
        lyr((d, d)), lyr((N_EXPERTS, d)),
    ]
    args = [zq, zs, x, mod5, mod5, mod5,
            p["g_q"], p["g_k"], p["g_mla_q"], p["g_mla_kv"], p["conv_w"], p["g_grp"], p["g_ffn"],
            p["w_uq"], p["w_ukv"], p["w_out"], p["w_router_t"]]
    if latent:
        cspec = lambda w: pl.BlockSpec((None, None, past, w), lambda b, j: (b, layer, 0, 0))
        in_specs += [cspec(KV_W), cspec(KV_W), cspec(MLA_KV_RANK), cspec(MLA_ROPE)]
        in_specs += [const((seq, LANES))] * 4
        args += list(cache) + list(rope)
    out_specs = [pl.BlockSpec((qb, d), orow), pl.BlockSpec((qb, d), orow),
                 pl.BlockSpec((N_EXPERTS, qb), lambda b, j: (0, b * nq + j))]
    out_shape = [jax.ShapeDtypeStruct((t_out, d), F32), jax.ShapeDtypeStruct((t_out, d), F32),
                 jax.ShapeDtypeStruct((N_EXPERTS, t_out), F32)]
    if not latent:
        sspec = lambda w: pl.BlockSpec((None, seq, w), lambda b, j: (b, 0, 0))
        out_specs += [sspec(KV_W), sspec(KV_W), sspec(MLA_KV_RANK), sspec(MLA_ROPE)]
        out_shape += [jax.ShapeDtypeStruct((n_seq, seq, w), F32)
                      for w in (KV_W, KV_W, MLA_KV_RANK, MLA_ROPE)]
    scratch = [pltpu.VMEM((sk, KV_W), BF16), pltpu.VMEM((sk, KV_W), BF16),
               pltpu.VMEM((sk, MLA_HEADS * MLA_KVH), BF16), pltpu.VMEM((sk, LANES), BF16),
               pltpu.VMEM((seq, CONV_CH), F32), pltpu.VMEM((seq + 2 * CONV_HALO, CONV_CH), F32)]
    return pl.pallas_call(
        functools.partial(_mixer_kernel, seq=seq, past=past, qb=qb, latent=latent),
        grid=(n_seq, nq),
        in_specs=in_specs, out_specs=out_specs, out_shape=out_shape, scratch_shapes=scratch,
        compiler_params=_cparams(("parallel", "arbitrary")),
        name="mixer_latent" if latent else "mixer_context",
    )(*args)


def _route(logits, bias):
    tc = logits.shape[1]
    shape3 = (N_GROUPS, GROUP_SIZE, tc)
    scores = jax.nn.sigmoid(logits)
    choice = (scores + bias).reshape(shape3)
    scores = scores.reshape(shape3)
    neg = -jnp.inf
    member = lax.broadcasted_iota(jnp.int32, shape3, 1)
    m1 = jnp.max(choice, axis=1, keepdims=True)
    first = jnp.min(jnp.where(choice == m1, member, GROUP_SIZE), axis=1, keepdims=True)
    m2 = jnp.max(jnp.where(member == first, neg, choice), axis=1, keepdims=True)
    gscore = m1 + m2
    gid = lax.broadcasted_iota(jnp.int32, gscore.shape, 0)
    gmask = jnp.zeros(gscore.shape, jnp.int32)
    for _ in range(TOPK_GROUPS):
        m = jnp.max(gscore, axis=0, keepdims=True)
        pick = jnp.min(jnp.where(gscore == m, gid, N_GROUPS), axis=0, keepdims=True)
        hit = gid == pick
        gmask = jnp.where(hit, 1, gmask)
        gscore = jnp.where(hit, neg, gscore)
    eid = lax.broadcasted_iota(jnp.int32, shape3, 0) * GROUP_SIZE + member
    cur = jnp.where(gmask > 0, choice, neg)
    sel = jnp.zeros(shape3, jnp.int32)
    picks = []
    for _ in range(TOP_K):
        m = jnp.max(jnp.max(cur, axis=0, keepdims=True), axis=1, keepdims=True)
        cand = jnp.where(cur == m, eid, N_EXPERTS)
        pick = jnp.min(jnp.min(cand, axis=0, keepdims=True), axis=1, keepdims=True)
        hit = eid == pick
        sel = jnp.where(hit, 1, sel)
        cur = jnp.where(hit, neg, cur)
        picks.append(pick)
    w = jnp.where(sel > 0, scores, 0.0)
    wsum = jnp.sum(jnp.sum(w, axis=0, keepdims=True), axis=1, keepdims=True)
    return eid, picks, sel, w / wsum * ROUTED_SCALE


def _sum_experts(x3):
    return jnp.sum(jnp.sum(x3, axis=0, keepdims=True), axis=1, keepdims=True)


def _router_kernel(lg_ref, b_ref, pos_ref, w_ref, cnt_out_ref, cnt_ref, base_ref, *, tm):
    phase, i = pl.program_id(0), pl.program_id(1)
    tc = lg_ref.shape[1]
    eid, picks, sel, comb = _route(lg_ref[...], b_ref[...])
    sel = sel.astype(F32).reshape(N_EXPERTS, tc)

    @pl.when(phase == 0)
    def _count():
        @pl.when(i == 0)
        def _():
            cnt_ref[...] = jnp.zeros_like(cnt_ref)
        cnt_ref[...] += jnp.sum(sel, axis=1, keepdims=True)
        cnt_out_ref[...] = jnp.broadcast_to(cnt_ref[...], cnt_out_ref.shape)
        pos_ref[...] = jnp.zeros_like(pos_ref)
        w_ref[...] = jnp.zeros_like(w_ref)

    @pl.when(phase == 1)
    def _assign():
        @pl.when(i == 0)
        def _():
            tiles = jnp.floor((cnt_ref[...] + (tm - 1)) * (1.0 / tm))
            r = lax.broadcasted_iota(I32, (N_EXPERTS, N_EXPERTS), 0)
            c = lax.broadcasted_iota(I32, (N_EXPERTS, N_EXPERTS), 1)
            lower = (c < r).astype(BF16)
            first_tile = _dot(lower, jnp.broadcast_to(tiles, (N_EXPERTS, LANES)).astype(BF16))
            base_ref[...] = first_tile[:, :1] * tm
        r = lax.broadcasted_iota(I32, (tc, tc), 0)
        c = lax.broadcasted_iota(I32, (tc, tc), 1)
        incl = _dot(sel.astype(BF16), (r <= c).astype(BF16))
        row = (base_ref[...] + incl - 1.0).reshape(eid.shape)
        base_ref[...] += jnp.sum(sel, axis=1, keepdims=True)
        pos, wts = [], []
        for pick in picks:
            hit = eid == pick
            pos.append(_sum_experts(jnp.where(hit, row, 0.0)).reshape(1, tc))
            wts.append(_sum_experts(jnp.where(hit, comb, 0.0)).reshape(1, tc))
        pos_ref[...] = jnp.concatenate(pos, axis=0).astype(I32)
        wts = jnp.concatenate(wts + [jnp.zeros((LANES - TOP_K, tc), F32)], axis=0)
        w_ref[...] = wts.T


def _router(logits_t, b_router, layer, tm, tc=512):
    e, t = logits_t.shape
    return pl.pallas_call(
        functools.partial(_router_kernel, tm=tm),
        grid=(2, t // tc),
        in_specs=[pl.BlockSpec((e, tc), lambda p, i: (0, i)),
                  pl.BlockSpec((None, e, 1), lambda p, i: (layer, 0, 0))],
        out_specs=[pl.BlockSpec((TOP_K, tc), lambda p, i: (0, p * i)),
                   pl.BlockSpec((tc, LANES), lambda p, i: (p * i, 0)),
                   pl.BlockSpec((e, LANES), lambda p, i: (0, 0))],
        out_shape=[jax.ShapeDtypeStruct((TOP_K, t), I32), jax.ShapeDtypeStruct((t, LANES), F32),
                   jax.ShapeDtypeStruct((e, LANES), F32)],
        scratch_shapes=[pltpu.VMEM((e, 1), F32), pltpu.VMEM((e, 1), F32)],
        compiler_params=_cparams(("arbitrary", "arbitrary")),
        name="router",
    )(logits_t, b_router.reshape(-1, e, 1))


def _tile_plan(counts, tm, n_tiles):
    tiles = (counts.astype(I32) + (tm - 1)) // tm
    ends = jnp.cumsum(tiles)
    n_active = ends[-1]
    tile = jnp.arange(n_tiles, dtype=I32)
    expert = jnp.sum((tile[:, None] >= ends[None, :]).astype(I32), axis=1)
    expert = jnp.where(tile < n_active, expert, expert[n_active - 1])
    last_tile_row = (ends - 1) * tm
    return expert, n_active.reshape(1), last_tile_row.astype(I32), (tiles > 0).astype(I32)


def _dispatch_kernel(zrow_ref, zflag_ref, h_ref, pos_ref, xs_ref, zero_ref, pos_smem,
                     row_sem, zero_sem, pos_sem, *, tm):
    tt = h_ref.shape[0]

    def zero_copy(e):
        return pltpu.make_async_copy(
            zero_ref, xs_ref.at[pl.ds(pl.multiple_of(zrow_ref[e], tm), tm), :], zero_sem)

    @pl.when(pl.program_id(0) == 0)
    def _zero_partial_tiles():
        zero_ref[...] = jnp.zeros_like(zero_ref)
        for e in range(N_EXPERTS):
            @pl.when(zflag_ref[e] > 0)
            def _():
                zero_copy(e).start()
        for e in range(N_EXPERTS):
            @pl.when(zflag_ref[e] > 0)
            def _():
                zero_copy(e).wait()

    pos_copy = pltpu.make_async_copy(pos_ref, pos_smem, pos_sem)
    pos_copy.start()
    pos_copy.wait()

    def issue(t, carry):
        for k in range(TOP_K):
            pltpu.make_async_copy(h_ref.at[pl.ds(t, 1), :],
                                  xs_ref.at[pl.ds(pos_smem[k, t], 1), :], row_sem).start()
        return carry

    lax.fori_loop(0, tt, issue, 0, unroll=8)
    for _ in range(TOP_K):
        pltpu.make_async_copy(h_ref, xs_ref.at[pl.ds(0, tt), :], row_sem).wait()


def _dispatch(h, pos, zrow, zflag, n_rows, tm, tt=256):
    t, d = h.shape
    return pl.pallas_call(
        functools.partial(_dispatch_kernel, tm=tm),
        grid_spec=pltpu.PrefetchScalarGridSpec(
            num_scalar_prefetch=2,
            grid=(t // tt,),
            in_specs=[pl.BlockSpec((tt, d), lambda i, *_: (i, 0)),
                      pl.BlockSpec((TOP_K, tt), lambda i, *_: (0, i))],
            out_specs=pl.BlockSpec(memory_space=pl.ANY),
            scratch_shapes=[pltpu.VMEM((tm, d), F32), pltpu.SMEM((TOP_K, tt), I32),
                            pltpu.SemaphoreType.DMA, pltpu.SemaphoreType.DMA, pltpu.SemaphoreType.DMA],
        ),
        out_shape=jax.ShapeDtypeStruct((n_rows, d), F32),
        compiler_params=_cparams(("arbitrary",)),
        name="moe_dispatch",
    )(zrow, zflag, h, pos)


def _expert_kernel(te_ref, na_ref, xs_ref, wg_ref, wu_ref, wd_ref, o_ref, wg_bf, wu_bf, wd_bf):
    i = pl.program_id(0)

    @pl.when(i < na_ref[0])
    def _():
        @pl.when(jnp.logical_or(i == 0, te_ref[i] != te_ref[jnp.maximum(i - 1, 0)]))
        def _():
            wg_bf[...] = wg_ref[...].astype(BF16)
            wu_bf[...] = wu_ref[...].astype(BF16)
            wd_bf[...] = wd_ref[...].astype(BF16)

        x = xs_ref[...].astype(BF16)
        hg = _dot(x, wg_bf[...])
        hu = _dot(x, wu_bf[...])
        o_ref[...] = _dot((_silu(hg) * hu).astype(BF16), wd_bf[...])


def _experts(xs, tile_expert, n_active, w_gate, w_up, w_down, layer, tm):
    n_rows = xs.shape[0]
    d, f = w_gate.shape[-2:]
    row = lambda i, te, na: (jnp.minimum(i, na[0] - 1), 0)
    wspec = lambda shape: pl.BlockSpec((None, None) + shape, lambda i, te, na: (layer, te[i], 0, 0))
    return pl.pallas_call(
        _expert_kernel,
        grid_spec=pltpu.PrefetchScalarGridSpec(
            num_scalar_prefetch=2,
            grid=(n_rows // tm,),
            in_specs=[pl.BlockSpec((tm, d), row), wspec((d, f)), wspec((d, f)), wspec((f, d))],
            out_specs=pl.BlockSpec((tm, d), row),
            scratch_shapes=[pltpu.VMEM((d, f), BF16), pltpu.VMEM((d, f), BF16), pltpu.VMEM((f, d), BF16)],
        ),
        out_shape=jax.ShapeDtypeStruct((n_rows, d), F32),
        compiler_params=_cparams(("arbitrary",)),
        name="moe_experts",
    )(tile_expert, n_active, xs, w_gate, w_up, w_down)


def _combine_kernel(pos_ref, w_ref, h_ref, x_ref, gt2_ref, sg_ref, su_ref, sd_ref, gfin_ref, ys_ref,
                    o_ref, buf_ref, pos_smem, row_sem, pos_sem, *, final):
    tt = x_ref.shape[0]
    pos_copy = pltpu.make_async_copy(pos_ref, pos_smem, pos_sem)
    pos_copy.start()
    pos_copy.wait()

    def issue(t, carry):
        for k in range(TOP_K):
            pltpu.make_async_copy(ys_ref.at[pl.ds(pos_smem[k, t], 1), :],
                                  buf_ref.at[k, pl.ds(t, 1), :], row_sem).start()
        return carry

    lax.fori_loop(0, tt, issue, 0, unroll=8)
    h = h_ref[...].astype(BF16)
    acc = _dot((_silu(_dot(h, sg_ref[...])) * _dot(h, su_ref[...])).astype(BF16), sd_ref[...])
    w = w_ref[...]
    for k in range(TOP_K):
        pltpu.make_async_copy(ys_ref.at[pl.ds(0, tt), :], buf_ref.at[k], row_sem).wait()
    for k in range(TOP_K):
        acc = acc + buf_ref[k] * w[:, k:k + 1]
    y = x_ref[...] + gt2_ref[...] * acc
    o_ref[...] = _rms(y, gfin_ref[...]) if final else y


def _combine(pos, w, h2, x, mod5, ys, p, g_final, layer, row_fn, final, tt=128):
    t, d = x.shape
    f = p["ws_gate"].shape[-1]
    tok = lambda i: (i, 0)
    lyr = lambda shape: _resident((None,) + shape, lambda i: (layer,) + (0,) * len(shape))
    return pl.pallas_call(
        functools.partial(_combine_kernel, final=final),
        grid=(t // tt,),
        in_specs=[
            pl.BlockSpec((TOP_K, tt), lambda i: (0, i)), pl.BlockSpec((tt, LANES), tok),
            pl.BlockSpec((tt, d), tok), pl.BlockSpec((tt, d), tok),
            pl.BlockSpec((None, None, None, 1, d), lambda i: (layer, row_fn(i), 5, 0, 0)),
            lyr((d, f)), lyr((d, f)), lyr((f, d)),
            _resident((1, d), lambda i: (0, 0)),
            pl.BlockSpec(memory_space=pl.ANY),
        ],
        out_specs=pl.BlockSpec((tt, d), tok),
        out_shape=jax.ShapeDtypeStruct((t, d), F32),
        scratch_shapes=[pltpu.VMEM((TOP_K, tt, d), F32), pltpu.SMEM((TOP_K, tt), I32),
                        pltpu.SemaphoreType.DMA, pltpu.SemaphoreType.DMA],
        compiler_params=_cparams(("arbitrary",)),
        name="moe_combine",
    )(pos, w, h2, x, mod5, p["ws_gate"], p["ws_up"], p["ws_down"], g_final.reshape(1, d), ys)


def _rope_tables(n_tokens, dim):
    rows = n_tokens // GRID_W
    row = jnp.repeat(jnp.arange(rows, dtype=jnp.int32), GRID_W).astype(F32)
    col = jnp.tile(jnp.arange(GRID_W, dtype=jnp.int32), rows).astype(F32)
    half = dim // 2
    inv_freq = ROPE_THETA ** (-(jnp.arange(half // 2, dtype=F32) * 2.0 / half))
    ang_r = row[:, None] * inv_freq[None, :]
    ang_c = col[:, None] * inv_freq[None, :]
    cos = jnp.concatenate([jnp.cos(ang_r)] * 2 + [jnp.cos(ang_c)] * 2, axis=-1)
    sin = jnp.concatenate([-jnp.sin(ang_r), jnp.sin(ang_r), -jnp.sin(ang_c), jnp.sin(ang_c)], axis=-1)
    pad = ((0, 0), (0, LANES - dim))
    return jnp.pad(cos, pad), jnp.pad(sin, pad)


def _prep_params(w_in, w_uq, w_ukv, w_out, w_router, w_gate, w_up, w_down, ws_gate, ws_up, ws_down):
    depth = w_in.shape[0]
    q, k, v, cb, cc, ch, cq, ckv, kr = jnp.split(
        w_in, [768, 1024, 1280, 1792, 2304, 2816, 3328, 3584], axis=-1)
    kr = jnp.pad(kr, ((0, 0), (0, 0), (0, LANES - MLA_ROPE)))
    w_in_r = jnp.concatenate([q, cq, cb, k, v, ckv, cc, ch, kr], axis=-1).astype(BF16)
    w_uq_r = jnp.pad(w_uq.reshape(depth, MLA_Q_RANK, MLA_HEADS, MLA_NOPE + MLA_ROPE),
                     ((0, 0), (0, 0), (0, 0), (0, MLA_QH - MLA_NOPE - MLA_ROPE)))
    w_uq_r = w_uq_r.reshape(depth, MLA_Q_RANK, MLA_HEADS * MLA_QH).astype(BF16)
    return {
        "w_in": w_in_r, "w_uq": w_uq_r, "w_ukv": w_ukv.astype(BF16), "w_out": w_out.astype(BF16),
        "w_router_t": jnp.swapaxes(w_router, 1, 2),
        "w_gate": w_gate, "w_up": w_up, "w_down": w_down,
        "ws_gate": ws_gate.astype(BF16), "ws_up": ws_up.astype(BF16), "ws_down": ws_down.astype(BF16),
    }


def kernel(x_prompt, x_sample, c, cache_gqa_k, cache_gqa_v, cache_mla_ckv, cache_mla_kpe, c_ctx, w_mod, b_mod, g_mix, w_in, g_q, g_k, conv_w, g_mla_q, g_mla_kv, w_uq, w_ukv, g_grp, w_out, g_ffn, w_router, b_router, w_gate, w_up, w_down, ws_gate, ws_up, ws_down, g_final):
    batch, seq, d = x_prompt.shape
    dec_batch, dec_seq, _ = x_sample.shape
    depth = w_mod.shape[0]
    past = cache_gqa_k.shape[2]
    t_ctx, t_lat = batch * seq, dec_batch * dec_seq
    tm = 256
    assert seq % tm == 0 and dec_seq % tm == 0 and t_ctx % dec_seq == 0
    n_moe_tiles = (t_ctx + t_lat) * TOP_K // MOE_TM + N_EXPERTS

    p = _prep_params(w_in, w_uq, w_ukv, w_out, w_router, w_gate, w_up, w_down, ws_gate, ws_up, ws_down)
    for name, val in (("g_q", g_q), ("g_k", g_k), ("g_mla_q", g_mla_q), ("g_mla_kv", g_mla_kv),
                      ("g_grp", g_grp), ("g_ffn", g_ffn)):
        p[name] = val.reshape(depth, 1, -1)
    p["conv_w"] = conv_w

    mod_rows = 8
    cond = jnp.zeros((mod_rows, d), F32).at[0].set(c_ctx).at[1:1 + dec_batch].set(c)
    mod5 = _modulation(cond, w_mod, b_mod).reshape(depth, mod_rows, 6, 1, d)

    def tile_row_fn(tile):
        n_ctx = t_ctx // tile
        per_seq = dec_seq // tile
        return lambda i: jnp.where(i < n_ctx, 0, 1 + (i - n_ctx) // per_seq)

    cache = (cache_gqa_k.reshape(dec_batch, depth, past, KV_W),
             cache_gqa_v.reshape(dec_batch, depth, past, KV_W), cache_mla_ckv, cache_mla_kpe)
    rope = _rope_tables(dec_seq, HEAD_DIM) + _rope_tables(dec_seq, MLA_ROPE)

    x = jnp.concatenate([x_prompt.reshape(t_ctx, d), x_sample.reshape(t_lat, d)], axis=0)
    new_ctx = []
    for layer in range(depth):
        zq, zs = _in_proj(x, mod5, g_mix, p["w_in"], layer, tile_row_fn(tm), tm)
        xc, h2c, lgc, nk, nv, nckv, nkpe = _mixer(
            zq, zs, x, mod5, p, layer, n_seq=batch, seq=seq, row0=0, mod_row_fn=lambda b: 0)
        xl, h2l, lgl = _mixer(
            zq, zs, x, mod5, p, layer, n_seq=dec_batch, seq=dec_seq, row0=t_ctx,
            mod_row_fn=lambda b: 1 + b, cache=cache, rope=rope, qb=128)
        new_ctx.append((nk, nv, nckv, nkpe))
        x_mid = jnp.concatenate([xc, xl], axis=0)
        h2 = jnp.concatenate([h2c, h2l], axis=0)
        logits_t = jnp.concatenate([lgc, lgl], axis=1)
        pos, w_tok, counts = _router(logits_t, b_router, layer, MOE_TM)
        tile_expert, n_active, zrow, zflag = _tile_plan(counts[:, 0], MOE_TM, n_moe_tiles)
        xs = _dispatch(h2, pos, zrow, zflag, n_moe_tiles * MOE_TM, MOE_TM)
        ys = _experts(xs, tile_expert, n_active, p["w_gate"], p["w_up"], p["w_down"], layer, MOE_TM)
        x = _combine(pos, w_tok, h2, x_mid, mod5, ys, p, g_final, layer, tile_row_fn(128),
                     final=layer == depth - 1)
    y = x
    y_prompt = y[:t_ctx].reshape(batch, seq, d)
    y_sample = y[t_ctx:].reshape(dec_batch, dec_seq, d)
    stack = lambda i: jnp.stack([lc[i] for lc in new_ctx], axis=1)
    new_k = stack(0).reshape(batch, depth, seq, GQA_KV_HEADS, HEAD_DIM)
    new_v = stack(1).reshape(batch, depth, seq, GQA_KV_HEADS, HEAD_DIM)
    return (y_prompt, y_sample, new_k, new_v, stack(2), stack(3))
```

```python
import functools

import jax
import jax.numpy as jnp
from jax import lax
from jax.experimental import pallas as pl
from jax.experimental.pallas import tpu as pltpu

F32 = jnp.float32
BF16 = jnp.bfloat16
I32 = jnp.int32

EPS = 1e-6
ROPE_THETA = 10000.0
GRID_W = 64
GQA_HEADS = 6
GQA_KV_HEADS = 2
HEAD_DIM = 128
CONV_CH = 512
MLA_HEADS = 6
MLA_Q_RANK = 512
MLA_KV_RANK = 256
MLA_NOPE = 128
MLA_ROPE = 64
MLA_V = 128
GQA_WIDTH = GQA_HEADS * HEAD_DIM
MLA_WIDTH = MLA_HEADS * MLA_V
N_EXPERTS = 64
TOP_K = 8
N_GROUPS = 8
TOPK_GROUPS = 4
GROUP_SIZE = N_EXPERTS // N_GROUPS
ROUTED_SCALE = 2.5
MOE_TM = 256
IN_TM = 256
COMBINE_TT = 128

LANES = 128
N_DMA_PRIORITIES = 2
VMEM_LIMIT_BYTES = 56 * 1024 * 1024

ZQ_W = GQA_WIDTH + MLA_Q_RANK + CONV_CH
KV_W = GQA_KV_HEADS * HEAD_DIM
ZS_W = 2 * KV_W + MLA_KV_RANK + 2 * CONV_CH + LANES
ZS_K, ZS_V, ZS_CKV = 0, KV_W, 2 * KV_W
ZS_CC = ZS_CKV + MLA_KV_RANK
ZS_CH = ZS_CC + CONV_CH
ZS_KR = ZS_CH + CONV_CH
_IN_Q, _IN_K, _IN_V, _IN_CB, _IN_CC, _IN_CH, _IN_CQ, _IN_CKV, _IN_KR = (
    0, 768, 1024, 1280, 1792, 2304, 2816, 3328, 3584)
W_IN_PAD = _IN_KR + LANES
ZQ_PIECES = ((_IN_Q, GQA_WIDTH), (_IN_CQ, MLA_Q_RANK), (_IN_CB, CONV_CH))
ZS_PIECES = ((_IN_K, 2 * KV_W), (_IN_CKV, MLA_KV_RANK), (_IN_CC, 2 * CONV_CH), (_IN_KR, LANES))
CONV_HALO = 8
MLA_QH = 2 * LANES
MLA_KVH = MLA_NOPE + MLA_V


def _cparams(sem, vmem=VMEM_LIMIT_BYTES):
    return pltpu.CompilerParams(dimension_semantics=sem, vmem_limit_bytes=vmem)


def _resident(shape, index_map):
    return pl.BlockSpec(shape, index_map, pipeline_mode=pl.Buffered(1))


def _rms(x, g):
    ms = jnp.mean(x * x, axis=-1, keepdims=True)
    return x * lax.rsqrt(ms + EPS) * g


def _dot(a, b):
    return jnp.dot(a, b, preferred_element_type=F32)


def _dot_nt(a, b):
    return lax.dot_general(a, b, (((1,), (1,)), ((), ())), preferred_element_type=F32)


def _silu(x):
    return x * jax.nn.sigmoid(x)


def _rope(x, cos, sin_signed, hb):
    lane = lax.broadcasted_iota(jnp.int32, x.shape, 1)
    partner = jnp.where((lane % (2 * hb)) < hb,
                        pltpu.roll(x, LANES - hb, axis=1), pltpu.roll(x, hb, axis=1))
    return x * cos + partner * sin_signed


def _mod_kernel(c_ref, w_ref, b_ref, o_ref):
    c = c_ref[...]
    a = (c * jax.nn.sigmoid(c)).astype(BF16)
    o_ref[...] = _dot(a, w_ref[...].astype(BF16)) + b_ref[...]


def _modulation(cond, w_mod, b_mod, tn=1024):
    depth, d, n = w_mod.shape
    rows = cond.shape[0]
    return pl.pallas_call(
        _mod_kernel,
        grid=(depth, n // tn),
        in_specs=[
            pl.BlockSpec((rows, d), lambda l, j: (0, 0)),
            pl.BlockSpec((None, d, tn), lambda l, j: (l, 0, j)),
            pl.BlockSpec((None, 1, tn), lambda l, j: (l, 0, j)),
        ],
        out_specs=pl.BlockSpec((None, rows, tn), lambda l, j: (l, 0, j)),
        out_shape=jax.ShapeDtypeStruct((depth, rows, n), F32),
        compiler_params=_cparams(("parallel", "parallel")),
        name="modulation",
    )(cond, w_mod, b_mod.reshape(depth, 1, n))


def _in_proj_kernel(x_ref, g_ref, sh_ref, sc_ref, w_ref, zq_ref, zs_ref):
    h = _rms(x_ref[...], g_ref[...]) * (1.0 + sc_ref[...]) + sh_ref[...]
    z = _dot(h.astype(BF16), w_ref[...])
    for ref, pieces in ((zq_ref, ZQ_PIECES), (zs_ref, ZS_PIECES)):
        dst = 0
        for src, width in pieces:
            ref[:, dst:dst + width] = z[:, src:src + width]
            dst += width


def _in_proj(x, mod5, g_mix, w_in, layer, row_fn, tm):
    t, d = x.shape
    n = w_in.shape[-1]
    mspec = lambda chunk: pl.BlockSpec((None, None, None, 1, d),
                                       lambda i: (layer, row_fn(i), chunk, 0, 0))
    return pl.pallas_call(
        _in_proj_kernel,
        grid=(t // tm,),
        in_specs=[
            pl.BlockSpec((tm, d), lambda i: (i, 0)),
            pl.BlockSpec((None, 1, d), lambda i: (layer, 0, 0)),
            mspec(0), mspec(1),
            _resident((None, d, n), lambda i: (layer, 0, 0)),
        ],
        out_specs=[pl.BlockSpec((tm, ZQ_W), lambda i: (i, 0)),
                   pl.BlockSpec((tm, ZS_W), lambda i: (i, 0))],
        out_shape=[jax.ShapeDtypeStruct((t, ZQ_W), F32), jax.ShapeDtypeStruct((t, ZS_W), F32)],
        compiler_params=_cparams(("parallel",)),
        name="in_proj",
    )(x, g_mix.reshape(-1, 1, d), mod5, mod5, w_in)


def _mixer_kernel(*refs, seq, past, qb, latent):
    it = iter(refs)
    zq_ref, zs_ref, x_ref = next(it), next(it), next(it)
    gt1_ref, sh2_ref, sc2_ref = next(it), next(it), next(it)
    gq_ref, gk_ref, gmq_ref, gmkv_ref = next(it), next(it), next(it), next(it)
    convw_ref, ggrp_ref, gffn_ref = next(it), next(it), next(it)
    wuq_ref, wukv_ref, wout_ref, wrt_ref = next(it), next(it), next(it), next(it)
    if latent:
        ck_ref, cv_ref, cckv_ref, ckpe_ref = next(it), next(it), next(it), next(it)
        cosa_ref, sina_ref, cosc_ref, sinc_ref = next(it), next(it), next(it), next(it)
    xo_ref, h2_ref, lg_ref = next(it), next(it), next(it)
    if not latent:
        nk_ref, nv_ref, nckv_ref, nkpe_ref = next(it), next(it), next(it), next(it)
    kbf_ref, vbf_ref, kvm_ref, kpe_ref, conv_ref, u_ref = (next(it) for _ in range(6))

    j = pl.program_id(1)

    @pl.when(j == 0)
    def _per_sequence():
        if latent:
            kbf_ref[:past, :] = ck_ref[...].astype(BF16)
            vbf_ref[:past, :] = cv_ref[...].astype(BF16)
            kpe_ref[:past, :MLA_ROPE] = ckpe_ref[...].astype(BF16)
            kpe_ref[:past, MLA_ROPE:] = jnp.zeros((past, LANES - MLA_ROPE), BF16)
            for c0 in range(0, past, qb):
                c1 = min(c0 + qb, past)
                kvm_ref[c0:c1, :] = _dot(cckv_ref[c0:c1, :].astype(BF16), wukv_ref[...]).astype(BF16)
        u_ref[:CONV_HALO, :] = jnp.zeros((CONV_HALO, CONV_CH), F32)
        u_ref[CONV_HALO + seq:, :] = jnp.zeros((CONV_HALO, CONV_CH), F32)
        for c0 in range(0, seq, qb):
            rows, prow = slice(c0, c0 + qb), slice(past + c0, past + c0 + qb)
            for hk in range(GQA_KV_HEADS):
                sl = slice(hk * HEAD_DIM, (hk + 1) * HEAD_DIM)
                k = _rms(zs_ref[rows, ZS_K + hk * HEAD_DIM:ZS_K + (hk + 1) * HEAD_DIM], gk_ref[...])
                if latent:
                    k = _rope(k, cosa_ref[rows, :], sina_ref[rows, :], HEAD_DIM // 4)
                else:
                    nk_ref[rows, sl] = k
                kbf_ref[prow, sl] = k.astype(BF16)
            v = zs_ref[rows, ZS_V:ZS_V + KV_W]
            vbf_ref[prow, :] = v.astype(BF16)
            ckv_n = _rms(zs_ref[rows, ZS_CKV:ZS_CKV + MLA_KV_RANK], gmkv_ref[...])
            kpe = zs_ref[rows, ZS_KR:ZS_KR + LANES]
            if latent:
                kpe = _rope(kpe, cosc_ref[rows, :], sinc_ref[rows, :], MLA_ROPE // 4)
            else:
                nv_ref[rows, :] = v
                nckv_ref[rows, :] = ckv_n
                nkpe_ref[rows, :] = kpe[:, :MLA_ROPE]
            kpe_ref[prow, :] = kpe.astype(BF16)
            kvm_ref[prow, :] = _dot(ckv_n.astype(BF16), wukv_ref[...]).astype(BF16)
            u_ref[CONV_HALO + c0:CONV_HALO + c0 + qb, :] = (
                zs_ref[rows, ZS_CC:ZS_CC + CONV_CH] * zs_ref[rows, ZS_CH:ZS_CH + CONV_CH])
        for c0 in range(0, seq, qb):
            taps = [u_ref[CONV_HALO - 1 + c0 + i:CONV_HALO - 1 + c0 + i + qb, :] * convw_ref[i:i + 1, :]
                    for i in range(3)]
            conv_ref[c0:c0 + qb, :] = taps[0] + taps[1] + taps[2]

    r0 = pl.multiple_of(j * qb, qb)

    def attend(s, v_bf):
        m = jnp.max(s, axis=-1, keepdims=True)
        e = jnp.exp(s - m)
        return _dot(e.astype(BF16), v_bf) / jnp.sum(e, axis=-1, keepdims=True)

    outs_a = []
    for h in range(GQA_HEADS):
        hk = h // (GQA_HEADS // GQA_KV_HEADS)
        q = _rms(zq_ref[:, h * HEAD_DIM:(h + 1) * HEAD_DIM], gq_ref[...])
        if latent:
            q = _rope(q, cosa_ref[pl.ds(r0, qb), :], sina_ref[pl.ds(r0, qb), :], HEAD_DIM // 4)
        ksl = slice(hk * HEAD_DIM, (hk + 1) * HEAD_DIM)
        s = _dot_nt(q.astype(BF16), kbf_ref[:, ksl]) * (HEAD_DIM ** -0.5)
        outs_a.append(attend(s, vbf_ref[:, ksl]))
    out_a = jnp.concatenate(outs_a, axis=-1)

    cq_n = _rms(zq_ref[:, GQA_WIDTH:GQA_WIDTH + MLA_Q_RANK], gmq_ref[...])
    q_m = _dot(cq_n.astype(BF16), wuq_ref[...])
    outs_c = []
    for h in range(MLA_HEADS):
        q_nope = q_m[:, h * MLA_QH:h * MLA_QH + MLA_NOPE]
        q_pe = q_m[:, h * MLA_QH + MLA_NOPE:(h + 1) * MLA_QH]
        if latent:
            q_pe = _rope(q_pe, cosc_ref[pl.ds(r0, qb), :], sinc_ref[pl.ds(r0, qb), :], MLA_ROPE // 4)
        s = (_dot_nt(q_nope.astype(BF16), kvm_ref[:, h * MLA_KVH:h * MLA_KVH + MLA_NOPE])
             + _dot_nt(q_pe.astype(BF16), kpe_ref[...])) * ((MLA_NOPE + MLA_ROPE) ** -0.5)
        outs_c.append(attend(s, kvm_ref[:, h * MLA_KVH + MLA_NOPE:(h + 1) * MLA_KVH]))
    out_c = jnp.concatenate(outs_c, axis=-1)

    out_b = zq_ref[:, GQA_WIDTH + MLA_Q_RANK:] * conv_ref[pl.ds(r0, qb), :]

    g = ggrp_ref
    merged = jnp.concatenate([
        _rms(out_a, g[:, :GQA_WIDTH]).astype(BF16),
        _rms(out_b, g[:, GQA_WIDTH:GQA_WIDTH + CONV_CH]).astype(BF16),
        _rms(out_c, g[:, GQA_WIDTH + CONV_CH:]).astype(BF16)], axis=-1)
    x_new = x_ref[...] + gt1_ref[...] * _dot(merged, wout_ref[...])
    xo_ref[...] = x_new
    h2 = _rms(x_new, gffn_ref[...]) * (1.0 + sc2_ref[...]) + sh2_ref[...]
    h2_ref[...] = h2
    lg_ref[...] = lax.dot_general(wrt_ref[...], h2, (((1,), (1,)), ((), ())),
                                  precision=lax.Precision.HIGHEST, preferred_element_type=F32)


def _mixer(zq, zs, x, mod5, p, layer, *, n_seq, seq, mod_row_fn, cache=None, rope=None, qb=256):
    d = x.shape[1]
    t_out = n_seq * seq
    latent = cache is not None
    past = cache[0].shape[2] if latent else 0
    nq = seq // qb
    sk = past + seq

    def const(shape):
        return _resident(shape, lambda b, j: (0,) * len(shape))

    def lyr(shape):
        return _resident((None,) + shape, lambda b, j: (layer,) + (0,) * len(shape))

    mspec = lambda chunk: pl.BlockSpec((None, None, None, 1, d),
                                       lambda b, j: (layer, mod_row_fn(b), chunk, 0, 0))
    qrow = orow = lambda b, j: (b * nq + j, 0)
    seq_spec = _resident if nq > 1 else pl.BlockSpec
    in_specs = [
        pl.BlockSpec((qb, ZQ_W), qrow),
        seq_spec((seq, ZS_W), lambda b, j: (b, 0)),
        pl.BlockSpec((qb, d), qrow),
        mspec(2), mspec(3), mspec(4),
        lyr((1, HEAD_DIM)), lyr((1, HEAD_DIM)), lyr((1, MLA_Q_RANK)), lyr((1, MLA_KV_RANK)),
        lyr((3, CONV_CH)), lyr((1, d)), lyr((1, d)),
        lyr((MLA_Q_RANK, MLA_HEADS * MLA_QH)), lyr((MLA_KV_RANK, MLA_HEADS * MLA_KVH)),
        lyr((d, d)), lyr((N_EXPERTS, d)),
    ]
    args = [zq, zs, x, mod5, mod5, mod5,
            p["g_q"], p["g_k"], p["g_mla_q"], p["g_mla_kv"], p["conv_w"], p["g_grp"], p["g_ffn"],
            p["w_uq"], p["w_ukv"], p["w_out"], p["w_router_t"]]
    if latent:
        cspec = lambda w: pl.BlockSpec((None, None, past, w), lambda b, j: (b, layer, 0, 0))
        in_specs += [cspec(KV_W), cspec(KV_W), cspec(MLA_KV_RANK), cspec(MLA_ROPE)]
        in_specs += [const((seq, LANES))] * 4
        args += list(cache) + list(rope)
    out_specs = [pl.BlockSpec((qb, d), orow), pl.BlockSpec((qb, d), orow),
                 pl.BlockSpec((N_EXPERTS, qb), lambda b, j: (0, b * nq + j))]
    out_shape = [jax.ShapeDtypeStruct((t_out, d), F32), jax.ShapeDtypeStruct((t_out, d), F32),
                 jax.ShapeDtypeStruct((N_EXPERTS, t_out), F32)]
    if not latent:
        sspec = lambda w: pl.BlockSpec((None, seq, w), lambda b, j: (b, 0, 0))
        out_specs += [sspec(KV_W), sspec(KV_W), sspec(MLA_KV_RANK), sspec(MLA_ROPE)]
        out_shape += [jax.ShapeDtypeStruct((n_seq, seq, w), F32)
                      for w in (KV_W, KV_W, MLA_KV_RANK, MLA_ROPE)]
    scratch = [pltpu.VMEM((sk, KV_W), BF16), pltpu.VMEM((sk, KV_W), BF16),
               pltpu.VMEM((sk, MLA_HEADS * MLA_KVH), BF16), pltpu.VMEM((sk, LANES), BF16),
               pltpu.VMEM((seq, CONV_CH), F32), pltpu.VMEM((seq + 2 * CONV_HALO, CONV_CH), F32)]
    return pl.pallas_call(
        functools.partial(_mixer_kernel, seq=seq, past=past, qb=qb, latent=latent),
        grid=(n_seq, nq),
        in_specs=in_specs, out_specs=out_specs, out_shape=out_shape, scratch_shapes=scratch,
        compiler_params=_cparams(("parallel", "arbitrary")),
        name="mixer_latent" if latent else "mixer_context",
    )(*args)


def _route(logits, bias):
    tc = logits.shape[1]
    shape3 = (N_GROUPS, GROUP_SIZE, tc)
    scores = jax.nn.sigmoid(logits)
    choice = (scores + bias).reshape(shape3)
    scores = scores.reshape(shape3)
    neg = -jnp.inf
    member = lax.broadcasted_iota(jnp.int32, shape3, 1)
    m1 = jnp.max(choice, axis=1, keepdims=True)
    first = jnp.min(jnp.where(choice == m1, member, GROUP_SIZE), axis=1, keepdims=True)
    m2 = jnp.max(jnp.where(member == first, neg, choice), axis=1, keepdims=True)
    gscore = m1 + m2
    gid = lax.broadcasted_iota(jnp.int32, gscore.shape, 0)
    gmask = jnp.zeros(gscore.shape, jnp.int32)
    for _ in range(TOPK_GROUPS):
        m = jnp.max(gscore, axis=0, keepdims=True)
        pick = jnp.min(jnp.where(gscore == m, gid, N_GROUPS), axis=0, keepdims=True)
        hit = gid == pick
        gmask = jnp.where(hit, 1, gmask)
        gscore = jnp.where(hit, neg, gscore)
    eid = lax.broadcasted_iota(jnp.int32, shape3, 0) * GROUP_SIZE + member
    cur = jnp.where(gmask > 0, choice, neg)
    sel = jnp.zeros(shape3, jnp.int32)
    picks = []
    for _ in range(TOP_K):
        m = jnp.max(jnp.max(cur, axis=0, keepdims=True), axis=1, keepdims=True)
        cand = jnp.where(cur == m, eid, N_EXPERTS)
        pick = jnp.min(jnp.min(cand, axis=0, keepdims=True), axis=1, keepdims=True)
        hit = eid == pick
        sel = jnp.where(hit, 1, sel)
        cur = jnp.where(hit, neg, cur)
        picks.append(pick)
    w = jnp.where(sel > 0, scores, 0.0)
    wsum = jnp.sum(jnp.sum(w, axis=0, keepdims=True), axis=1, keepdims=True)
    return eid, picks, sel, w / wsum * ROUTED_SCALE


def _sum_experts(x3):
    return jnp.sum(jnp.sum(x3, axis=0, keepdims=True), axis=1, keepdims=True)


def _two_part_specs(block, n_first, axis, rest=0):
    def index(part_index):
        def index_map(*grid_ids):
            i = grid_ids[axis]
            idx = [0] * len(block)
            idx[len(block) - 1 - rest] = part_index(i)
            return tuple(idx)
        return index_map
    return [pl.BlockSpec(block, index(lambda i: jnp.minimum(i, n_first - 1))),
            pl.BlockSpec(block, index(lambda i: jnp.maximum(i - n_first, 0)))]


def _router_kernel(lgc_ref, lgl_ref, b_ref, pos_ref, w_ref, cnt_out_ref, cnt_ref, base_ref, *, tm, n_ctx):
    phase, i = pl.program_id(0), pl.program_id(1)
    tc = lgc_ref.shape[1]
    logits = jnp.where(i < n_ctx, lgc_ref[...], lgl_ref[...])
    eid, picks, sel, comb = _route(logits, b_ref[...])
    sel = sel.astype(F32).reshape(N_EXPERTS, tc)

    @pl.when(phase == 0)
    def _count():
        @pl.when(i == 0)
        def _():
            cnt_ref[...] = jnp.zeros_like(cnt_ref)
        cnt_ref[...] += jnp.sum(sel, axis=1, keepdims=True)
        cnt_out_ref[...] = jnp.broadcast_to(cnt_ref[...], cnt_out_ref.shape)
        pos_ref[...] = jnp.zeros_like(pos_ref)
        w_ref[...] = jnp.zeros_like(w_ref)

    @pl.when(phase == 1)
    def _assign():
        @pl.when(i == 0)
        def _():
            tiles = jnp.floor((cnt_ref[...] + (tm - 1)) * (1.0 / tm))
            r = lax.broadcasted_iota(I32, (N_EXPERTS, N_EXPERTS), 0)
            c = lax.broadcasted_iota(I32, (N_EXPERTS, N_EXPERTS), 1)
            lower = (c < r).astype(BF16)
            first_tile = _dot(lower, jnp.broadcast_to(tiles, (N_EXPERTS, LANES)).astype(BF16))
            base_ref[...] = first_tile[:, :1] * tm
        r = lax.broadcasted_iota(I32, (tc, tc), 0)
        c = lax.broadcasted_iota(I32, (tc, tc), 1)
        incl = _dot(sel.astype(BF16), (r <= c).astype(BF16))
        row = (base_ref[...] + incl - 1.0).reshape(eid.shape)
        base_ref[...] += jnp.sum(sel, axis=1, keepdims=True)
        pos, wts = [], []
        for pick in picks:
            hit = eid == pick
            pos.append(_sum_experts(jnp.where(hit, row, 0.0)).reshape(1, tc))
            wts.append(_sum_experts(jnp.where(hit, comb, 0.0)).reshape(1, tc))
        pos_ref[...] = jnp.concatenate(pos, axis=0).astype(I32)
        wts = jnp.concatenate(wts + [jnp.zeros((LANES - TOP_K, tc), F32)], axis=0)
        w_ref[...] = wts.T


def _router(logits_ctx, logits_lat, b_router, layer, tm, tc=512):
    e = logits_ctx.shape[0]
    n_ctx = logits_ctx.shape[1] // tc
    t = logits_ctx.shape[1] + logits_lat.shape[1]
    return pl.pallas_call(
        functools.partial(_router_kernel, tm=tm, n_ctx=n_ctx),
        grid=(2, t // tc),
        in_specs=_two_part_specs((e, tc), n_ctx, axis=1) + [
            pl.BlockSpec((None, e, 1), lambda p, i: (layer, 0, 0))],
        out_specs=[pl.BlockSpec((TOP_K, tc), lambda p, i: (0, p * i)),
                   pl.BlockSpec((tc, LANES), lambda p, i: (p * i, 0)),
                   pl.BlockSpec((e, LANES), lambda p, i: (0, 0))],
        out_shape=[jax.ShapeDtypeStruct((TOP_K, t), I32), jax.ShapeDtypeStruct((t, LANES), F32),
                   jax.ShapeDtypeStruct((e, LANES), F32)],
        scratch_shapes=[pltpu.VMEM((e, 1), F32), pltpu.VMEM((e, 1), F32)],
        compiler_params=_cparams(("arbitrary", "arbitrary")),
        name="router",
    )(logits_ctx, logits_lat, b_router.reshape(-1, e, 1))


def _tile_plan(counts, tm, n_tiles):
    tiles = (counts.astype(I32) + (tm - 1)) // tm
    ends = jnp.cumsum(tiles)
    n_active = ends[-1]
    tile = jnp.arange(n_tiles, dtype=I32)
    expert = jnp.sum((tile[:, None] >= ends[None, :]).astype(I32), axis=1)
    expert = jnp.where(tile < n_active, expert, expert[n_active - 1])
    last_tile_row = (ends - 1) * tm
    return expert, n_active.reshape(1), last_tile_row.astype(I32), (tiles > 0).astype(I32)


def _dispatch_kernel(zrow_ref, zflag_ref, hc_ref, hl_ref, pos_ref, xs_ref, h_ref, zero_ref, pos_smem,
                     row_sem, zero_sem, pos_sem, *, tm, n_ctx):
    tt = h_ref.shape[0]
    h_ref[...] = jnp.where(pl.program_id(0) < n_ctx, hc_ref[...], hl_ref[...])

    def zero_copy(e):
        return pltpu.make_async_copy(
            zero_ref, xs_ref.at[pl.ds(pl.multiple_of(zrow_ref[e], tm), tm), :], zero_sem)

    @pl.when(pl.program_id(0) == 0)
    def _zero_partial_tiles():
        zero_ref[...] = jnp.zeros_like(zero_ref)
        for e in range(N_EXPERTS):
            @pl.when(zflag_ref[e] > 0)
            def _():
                zero_copy(e).start()
        for e in range(N_EXPERTS):
            @pl.when(zflag_ref[e] > 0)
            def _():
                zero_copy(e).wait()

    pos_copy = pltpu.make_async_copy(pos_ref, pos_smem, pos_sem)
    pos_copy.start()
    pos_copy.wait()

    def issue(t, carry):
        for k in range(TOP_K):
            pltpu.make_async_copy(h_ref.at[pl.ds(t, 1), :], xs_ref.at[pl.ds(pos_smem[k, t], 1), :],
                                  row_sem).start(priority=k % N_DMA_PRIORITIES)
        return carry

    lax.fori_loop(0, tt, issue, 0, unroll=8)
    for _ in range(TOP_K):
        pltpu.make_async_copy(h_ref, xs_ref.at[pl.ds(0, tt), :], row_sem).wait()


def _dispatch(h_ctx, h_lat, pos, zrow, zflag, n_rows, tm, tt=256):
    d = h_ctx.shape[1]
    n_ctx = h_ctx.shape[0] // tt
    t = h_ctx.shape[0] + h_lat.shape[0]
    return pl.pallas_call(
        functools.partial(_dispatch_kernel, tm=tm, n_ctx=n_ctx),
        grid_spec=pltpu.PrefetchScalarGridSpec(
            num_scalar_prefetch=2,
            grid=(t // tt,),
            in_specs=_two_part_specs((tt, d), n_ctx, axis=0, rest=1) + [
                pl.BlockSpec((TOP_K, tt), lambda i, *_: (0, i))],
            out_specs=pl.BlockSpec(memory_space=pl.ANY),
            scratch_shapes=[pltpu.VMEM((tt, d), F32), pltpu.VMEM((tm, d), F32),
                            pltpu.SMEM((TOP_K, tt), I32),
                            pltpu.SemaphoreType.DMA, pltpu.SemaphoreType.DMA, pltpu.SemaphoreType.DMA],
        ),
        out_shape=jax.ShapeDtypeStruct((n_rows, d), F32),
        compiler_params=_cparams(("arbitrary",)),
        name="moe_dispatch",
    )(zrow, zflag, h_ctx, h_lat, pos)


def _expert_kernel(te_ref, na_ref, xs_ref, wg_ref, wu_ref, wd_ref, o_ref, wg_bf, wu_bf, wd_bf):
    i = pl.program_id(0)

    @pl.when(i < na_ref[0])
    def _():
        @pl.when(jnp.logical_or(i == 0, te_ref[i] != te_ref[jnp.maximum(i - 1, 0)]))
        def _():
            wg_bf[...] = wg_ref[...].astype(BF16)
            wu_bf[...] = wu_ref[...].astype(BF16)
            wd_bf[...] = wd_ref[...].astype(BF16)

        x = xs_ref[...].astype(BF16)
        hg = _dot(x, wg_bf[...])
        hu = _dot(x, wu_bf[...])
        o_ref[...] = _dot((_silu(hg) * hu).astype(BF16), wd_bf[...])


def _experts(xs, tile_expert, n_active, w_gate, w_up, w_down, layer, tm):
    n_rows = xs.shape[0]
    d, f = w_gate.shape[-2:]
    row = lambda i, te, na: (jnp.minimum(i, na[0] - 1), 0)
    wspec = lambda shape: pl.BlockSpec((None, None) + shape, lambda i, te, na: (layer, te[i], 0, 0))
    return pl.pallas_call(
        _expert_kernel,
        grid_spec=pltpu.PrefetchScalarGridSpec(
            num_scalar_prefetch=2,
            grid=(n_rows // tm,),
            in_specs=[pl.BlockSpec((tm, d), row), wspec((d, f)), wspec((d, f)), wspec((f, d))],
            out_specs=pl.BlockSpec((tm, d), row),
            scratch_shapes=[pltpu.VMEM((d, f), BF16), pltpu.VMEM((d, f), BF16), pltpu.VMEM((f, d), BF16)],
        ),
        out_shape=jax.ShapeDtypeStruct((n_rows, d), F32),
        compiler_params=_cparams(("arbitrary",)),
        name="moe_experts",
    )(tile_expert, n_active, xs, w_gate, w_up, w_down)


def _combine_kernel(pos_ref, w_ref, h_ref, x_ref, gt2_ref, sg_ref, su_ref, sd_ref, gfin_ref, ys_ref,
                    o_ref, buf_ref, pos_smem, row_sem, pos_sem, *, final):
    tt = x_ref.shape[0]
    pos_copy = pltpu.make_async_copy(pos_ref, pos_smem, pos_sem)
    pos_copy.start()
    pos_copy.wait()

    def issue(t, carry):
        for k in range(TOP_K):
            pltpu.make_async_copy(ys_ref.at[pl.ds(pos_smem[k, t], 1), :], buf_ref.at[k, pl.ds(t, 1), :],
                                  row_sem).start(priority=k % N_DMA_PRIORITIES)
        return carry

    lax.fori_loop(0, tt, issue, 0, unroll=8)
    h = h_ref[...].astype(BF16)
    acc = _dot((_silu(_dot(h, sg_ref[...])) * _dot(h, su_ref[...])).astype(BF16), sd_ref[...])
    w = w_ref[...]
    for k in range(TOP_K):
        pltpu.make_async_copy(ys_ref.at[pl.ds(0, tt), :], buf_ref.at[k], row_sem).wait()
    for k in range(TOP_K):
        acc = acc + buf_ref[k] * w[:, k:k + 1]
    y = x_ref[...] + gt2_ref[...] * acc
    o_ref[...] = _rms(y, gfin_ref[...]) if final else y


def _combine(pos, w, h2, x, mod5, ys, p, g_final, layer, row_fn, row0, final, tt=COMBINE_TT):
    t, d = x.shape
    f = p["ws_gate"].shape[-1]
    tile0 = row0 // tt
    tok = lambda i: (i, 0)
    lyr = lambda shape: _resident((None,) + shape, lambda i: (layer,) + (0,) * len(shape))
    return pl.pallas_call(
        functools.partial(_combine_kernel, final=final),
        grid=(t // tt,),
        in_specs=[
            pl.BlockSpec((TOP_K, tt), lambda i: (0, tile0 + i)),
            pl.BlockSpec((tt, LANES), lambda i: (tile0 + i, 0)),
            pl.BlockSpec((tt, d), tok), pl.BlockSpec((tt, d), tok),
            pl.BlockSpec((None, None, None, 1, d), lambda i: (layer, row_fn(i), 5, 0, 0)),
            lyr((d, f)), lyr((d, f)), lyr((f, d)),
            _resident((1, d), lambda i: (0, 0)),
            pl.BlockSpec(memory_space=pl.ANY),
        ],
        out_specs=pl.BlockSpec((tt, d), tok),
        out_shape=jax.ShapeDtypeStruct((t, d), F32),
        scratch_shapes=[pltpu.VMEM((TOP_K, tt, d), F32), pltpu.SMEM((TOP_K, tt), I32),
                        pltpu.SemaphoreType.DMA, pltpu.SemaphoreType.DMA],
        compiler_params=_cparams(("arbitrary",)),
        name="moe_combine",
    )(pos, w, h2, x, mod5, p["ws_gate"], p["ws_up"], p["ws_down"], g_final.reshape(1, d), ys)


def _rope_tables(n_tokens, dim):
    rows = n_tokens // GRID_W
    row = jnp.repeat(jnp.arange(rows, dtype=jnp.int32), GRID_W).astype(F32)
    col = jnp.tile(jnp.arange(GRID_W, dtype=jnp.int32), rows).astype(F32)
    half = dim // 2
    inv_freq = ROPE_THETA ** (-(jnp.arange(half // 2, dtype=F32) * 2.0 / half))
    ang_r = row[:, None] * inv_freq[None, :]
    ang_c = col[:, None] * inv_freq[None, :]
    cos = jnp.concatenate([jnp.cos(ang_r)] * 2 + [jnp.cos(ang_c)] * 2, axis=-1)
    sin = jnp.concatenate([-jnp.sin(ang_r), jnp.sin(ang_r), -jnp.sin(ang_c), jnp.sin(ang_c)], axis=-1)
    pad = ((0, 0), (0, LANES - dim))
    return jnp.pad(cos, pad), jnp.pad(sin, pad)


def _prep_params(w_in, w_uq, w_ukv, w_out, w_router, w_gate, w_up, w_down, ws_gate, ws_up, ws_down):
    depth = w_in.shape[0]
    w_in_p = jnp.pad(w_in.astype(BF16), ((0, 0), (0, 0), (0, W_IN_PAD - w_in.shape[-1])))
    w_uq_r = jnp.pad(w_uq.reshape(depth, MLA_Q_RANK, MLA_HEADS, MLA_NOPE + MLA_ROPE),
                     ((0, 0), (0, 0), (0, 0), (0, MLA_QH - MLA_NOPE - MLA_ROPE)))
    w_uq_r = w_uq_r.reshape(depth, MLA_Q_RANK, MLA_HEADS * MLA_QH).astype(BF16)
    return {
        "w_in": w_in_p, "w_uq": w_uq_r, "w_ukv": w_ukv.astype(BF16), "w_out": w_out.astype(BF16),
        "w_router_t": jnp.swapaxes(w_router, 1, 2),
        "w_gate": w_gate, "w_up": w_up, "w_down": w_down,
        "ws_gate": ws_gate.astype(BF16), "ws_up": ws_up.astype(BF16), "ws_down": ws_down.astype(BF16),
    }


def kernel(x_prompt, x_sample, c, cache_gqa_k, cache_gqa_v, cache_mla_ckv, cache_mla_kpe, c_ctx, w_mod, b_mod, g_mix, w_in, g_q, g_k, conv_w, g_mla_q, g_mla_kv, w_uq, w_ukv, g_grp, w_out, g_ffn, w_router, b_router, w_gate, w_up, w_down, ws_gate, ws_up, ws_down, g_final):
    batch, seq, d = x_prompt.shape
    dec_batch, dec_seq, _ = x_sample.shape
    depth = w_mod.shape[0]
    past = cache_gqa_k.shape[2]
    t_ctx, t_lat = batch * seq, dec_batch * dec_seq
    n_moe_tiles = (t_ctx + t_lat) * TOP_K // MOE_TM + N_EXPERTS

    p = _prep_params(w_in, w_uq, w_ukv, w_out, w_router, w_gate, w_up, w_down, ws_gate, ws_up, ws_down)
    for name, val in (("g_q", g_q), ("g_k", g_k), ("g_mla_q", g_mla_q), ("g_mla_kv", g_mla_kv),
                      ("g_grp", g_grp), ("g_ffn", g_ffn)):
        p[name] = val.reshape(depth, 1, -1)
    p["conv_w"] = conv_w

    mod_rows = 8
    cond = jnp.zeros((mod_rows, d), F32).at[0].set(c_ctx).at[1:1 + dec_batch].set(c)
    mod5 = _modulation(cond, w_mod, b_mod).reshape(depth, mod_rows, 6, 1, d)
    ctx_row = lambda i: 0
    lat_row = lambda tile: (lambda i: 1 + i // (dec_seq // tile))

    cache = (cache_gqa_k.reshape(dec_batch, depth, past, KV_W),
             cache_gqa_v.reshape(dec_batch, depth, past, KV_W), cache_mla_ckv, cache_mla_kpe)
    rope = _rope_tables(dec_seq, HEAD_DIM) + _rope_tables(dec_seq, MLA_ROPE)

    xc, xl = x_prompt.reshape(t_ctx, d), x_sample.reshape(t_lat, d)
    new_ctx = []
    for layer in range(depth):
        final = layer == depth - 1
        zqc, zsc = _in_proj(xc, mod5, g_mix, p["w_in"], layer, ctx_row, IN_TM)
        zql, zsl = _in_proj(xl, mod5, g_mix, p["w_in"], layer, lat_row(IN_TM), IN_TM)
        xc, h2c, lgc, nk, nv, nckv, nkpe = _mixer(
            zqc, zsc, xc, mod5, p, layer, n_seq=batch, seq=seq, mod_row_fn=ctx_row)
        xl, h2l, lgl = _mixer(
            zql, zsl, xl, mod5, p, layer, n_seq=dec_batch, seq=dec_seq,
            mod_row_fn=lambda b: 1 + b, cache=cache, rope=rope, qb=128)
        new_ctx.append((nk, nv, nckv, nkpe))
        pos, w_tok, counts = _router(lgc, lgl, b_router, layer, MOE_TM)
        tile_expert, n_active, zrow, zflag = _tile_plan(counts[:, 0], MOE_TM, n_moe_tiles)
        xs = _dispatch(h2c, h2l, pos, zrow, zflag, n_moe_tiles * MOE_TM, MOE_TM)
        ys = _experts(xs, tile_expert, n_active, p["w_gate"], p["w_up"], p["w_down"], layer, MOE_TM)
        xc = _combine(pos, w_tok, h2c, xc, mod5, ys, p, g_final, layer, ctx_row, 0, final)
        xl = _combine(pos, w_tok, h2l, xl, mod5, ys, p, g_final, layer, lat_row(COMBINE_TT), t_ctx, final)
    stack = lambda i: jnp.stack([lc[i] for lc in new_ctx], axis=1)
    new_k = stack(0).reshape(batch, depth, seq, GQA_KV_HEADS, HEAD_DIM)
    new_v = stack(1).reshape(batch, depth, seq, GQA_KV_HEADS, HEAD_DIM)
    return (xc.reshape(batch, seq, d), xl.reshape(dec_batch, dec_seq, d), new_k, new_v, stack(2), stack(3))
```

```python
import functools

import jax
import jax.numpy as jnp
from jax import lax
from jax.experimental import pallas as pl
from jax.experimental.pallas import tpu as pltpu

F32 = jnp.float32
BF16 = jnp.bfloat16
I32 = jnp.int32

EPS = 1e-6
ROPE_THETA = 10000.0
GRID_W = 64
GQA_HEADS = 6
GQA_KV_HEADS = 2
HEAD_DIM = 128
CONV_CH = 512
MLA_HEADS = 6
MLA_Q_RANK = 512
MLA_KV_RANK = 256
MLA_NOPE = 128
MLA_ROPE = 64
MLA_V = 128
GQA_WIDTH = GQA_HEADS * HEAD_DIM
MLA_WIDTH = MLA_HEADS * MLA_V
N_EXPERTS = 64
TOP_K = 8
N_GROUPS = 8
TOPK_GROUPS = 4
GROUP_SIZE = N_EXPERTS // N_GROUPS
ROUTED_SCALE = 2.5
MOE_TM = 256
IN_TM = 256
COMBINE_TT = 128

LANES = 128
SUBLANES = 8
N_DMA_PRIORITIES = 2
VMEM_LIMIT_BYTES = 56 * 1024 * 1024

ZQ_W = GQA_WIDTH + MLA_Q_RANK + CONV_CH
KV_W = GQA_KV_HEADS * HEAD_DIM
ZS_W = 2 * KV_W + MLA_KV_RANK + 2 * CONV_CH + LANES
ZS_K, ZS_V, ZS_CKV = 0, KV_W, 2 * KV_W
ZS_CC = ZS_CKV + MLA_KV_RANK
ZS_CH = ZS_CC + CONV_CH
ZS_KR = ZS_CH + CONV_CH
_IN_Q, _IN_K, _IN_V, _IN_CB, _IN_CC, _IN_CH, _IN_CQ, _IN_CKV, _IN_KR = (
    0, 768, 1024, 1280, 1792, 2304, 2816, 3328, 3584)
W_IN_PAD = _IN_KR + LANES
ZQ_PIECES = ((_IN_Q, GQA_WIDTH), (_IN_CQ, MLA_Q_RANK), (_IN_CB, CONV_CH))
ZS_PIECES = ((_IN_K, 2 * KV_W), (_IN_CKV, MLA_KV_RANK), (_IN_CC, 2 * CONV_CH), (_IN_KR, LANES))
CONV_HALO = 8
MLA_QH = 2 * LANES
MLA_KVH = MLA_NOPE + MLA_V


def _cparams(sem, vmem=VMEM_LIMIT_BYTES):
    return pltpu.CompilerParams(dimension_semantics=sem, vmem_limit_bytes=vmem)


def _resident(shape, index_map):
    return pl.BlockSpec(shape, index_map, pipeline_mode=pl.Buffered(1))


def _rms(x, g):
    ms = jnp.mean(x * x, axis=-1, keepdims=True)
    return x * lax.rsqrt(ms + EPS) * g


def _dot(a, b):
    return jnp.dot(a, b, preferred_element_type=F32)


def _dot_nt(a, b):
    return lax.dot_general(a, b, (((1,), (1,)), ((), ())), preferred_element_type=F32)


def _rows_to_slab(x):
    n, d = x.shape
    s = d // LANES
    y = jnp.stack([x[:, LANES * j:LANES * (j + 1)].reshape(n // SUBLANES, SUBLANES, LANES)
                   for j in range(s)], axis=1)
    return jnp.swapaxes(y, 1, 2).reshape(n, s, LANES)


def _slab_to_rows(x3):
    n, s, _ = x3.shape
    y = jnp.swapaxes(x3.reshape(n // SUBLANES, SUBLANES, s, LANES), 1, 2)
    return jnp.concatenate([y[:, j].reshape(n, LANES) for j in range(s)], axis=-1)


def _silu(x):
    return x * jax.nn.sigmoid(x)


def _rope(x, cos, sin_signed, hb):
    lane = lax.broadcasted_iota(jnp.int32, x.shape, 1)
    partner = jnp.where((lane % (2 * hb)) < hb,
                        pltpu.roll(x, LANES - hb, axis=1), pltpu.roll(x, hb, axis=1))
    return x * cos + partner * sin_signed


def _mod_kernel(c_ref, w_ref, b_ref, o_ref):
    c = c_ref[...]
    a = (c * jax.nn.sigmoid(c)).astype(BF16)
    o_ref[...] = _dot(a, w_ref[...].astype(BF16)) + b_ref[...]


def _modulation(cond, w_mod, b_mod, tn=1024):
    depth, d, n = w_mod.shape
    rows = cond.shape[0]
    return pl.pallas_call(
        _mod_kernel,
        grid=(depth, n // tn),
        in_specs=[
            pl.BlockSpec((rows, d), lambda l, j: (0, 0)),
            pl.BlockSpec((None, d, tn), lambda l, j: (l, 0, j)),
            pl.BlockSpec((None, 1, tn), lambda l, j: (l, 0, j)),
        ],
        out_specs=pl.BlockSpec((None, rows, tn), lambda l, j: (l, 0, j)),
        out_shape=jax.ShapeDtypeStruct((depth, rows, n), F32),
        compiler_params=_cparams(("parallel", "parallel")),
        name="modulation",
    )(cond, w_mod, b_mod.reshape(depth, 1, n))


def _in_proj_kernel(x_ref, g_ref, sh_ref, sc_ref, w_ref, zq_ref, zs_ref):
    h = _rms(x_ref[...], g_ref[...]) * (1.0 + sc_ref[...]) + sh_ref[...]
    z = _dot(h.astype(BF16), w_ref[...])
    for ref, pieces in ((zq_ref, ZQ_PIECES), (zs_ref, ZS_PIECES)):
        dst = 0
        for src, width in pieces:
            ref[:, dst:dst + width] = z[:, src:src + width]
            dst += width


def _in_proj(x, mod5, g_mix, w_in, layer, row_fn, tm):
    t, d = x.shape
    n = w_in.shape[-1]
    mspec = lambda chunk: pl.BlockSpec((None, None, None, 1, d),
                                       lambda i: (layer, row_fn(i), chunk, 0, 0))
    return pl.pallas_call(
        _in_proj_kernel,
        grid=(t // tm,),
        in_specs=[
            pl.BlockSpec((tm, d), lambda i: (i, 0)),
            pl.BlockSpec((None, 1, d), lambda i: (layer, 0, 0)),
            mspec(0), mspec(1),
            _resident((None, d, n), lambda i: (layer, 0, 0)),
        ],
        out_specs=[pl.BlockSpec((tm, ZQ_W), lambda i: (i, 0)),
                   pl.BlockSpec((tm, ZS_W), lambda i: (i, 0))],
        out_shape=[jax.ShapeDtypeStruct((t, ZQ_W), F32), jax.ShapeDtypeStruct((t, ZS_W), F32)],
        compiler_params=_cparams(("parallel",)),
        name="in_proj",
    )(x, g_mix.reshape(-1, 1, d), mod5, mod5, w_in)


def _mixer_kernel(*refs, seq, past, qb, latent):
    it = iter(refs)
    zq_ref, zs_ref, x_ref = next(it), next(it), next(it)
    gt1_ref, sh2_ref, sc2_ref = next(it), next(it), next(it)
    gq_ref, gk_ref, gmq_ref, gmkv_ref = next(it), next(it), next(it), next(it)
    convw_ref, ggrp_ref, gffn_ref = next(it), next(it), next(it)
    wuq_ref, wukv_ref, wout_ref, wrt_ref = next(it), next(it), next(it), next(it)
    if latent:
        ck_ref, cv_ref, cckv_ref, ckpe_ref = next(it), next(it), next(it), next(it)
        cosa_ref, sina_ref, cosc_ref, sinc_ref = next(it), next(it), next(it), next(it)
    xo_ref, h2_ref, lg_ref = next(it), next(it), next(it)
    if not latent:
        nk_ref, nv_ref, nckv_ref, nkpe_ref = next(it), next(it), next(it), next(it)
    kbf_ref, vbf_ref, kvm_ref, kpe_ref, conv_ref, u_ref = (next(it) for _ in range(6))

    j = pl.program_id(1)

    @pl.when(j == 0)
    def _per_sequence():
        if latent:
            kbf_ref[:past, :] = ck_ref[...].astype(BF16)
            vbf_ref[:past, :] = cv_ref[...].astype(BF16)
            kpe_ref[:past, :MLA_ROPE] = ckpe_ref[...].astype(BF16)
            kpe_ref[:past, MLA_ROPE:] = jnp.zeros((past, LANES - MLA_ROPE), BF16)
            for c0 in range(0, past, qb):
                c1 = min(c0 + qb, past)
                kvm_ref[c0:c1, :] = _dot(cckv_ref[c0:c1, :].astype(BF16), wukv_ref[...]).astype(BF16)
        u_ref[:CONV_HALO, :] = jnp.zeros((CONV_HALO, CONV_CH), F32)
        u_ref[CONV_HALO + seq:, :] = jnp.zeros((CONV_HALO, CONV_CH), F32)
        for c0 in range(0, seq, qb):
            rows, prow = slice(c0, c0 + qb), slice(past + c0, past + c0 + qb)
            for hk in range(GQA_KV_HEADS):
                sl = slice(hk * HEAD_DIM, (hk + 1) * HEAD_DIM)
                k = _rms(zs_ref[rows, ZS_K + hk * HEAD_DIM:ZS_K + (hk + 1) * HEAD_DIM], gk_ref[...])
                if latent:
                    k = _rope(k, cosa_ref[rows, :], sina_ref[rows, :], HEAD_DIM // 4)
                else:
                    nk_ref[rows, sl] = k
                kbf_ref[prow, sl] = k.astype(BF16)
            v = zs_ref[rows, ZS_V:ZS_V + KV_W]
            vbf_ref[prow, :] = v.astype(BF16)
            ckv_n = _rms(zs_ref[rows, ZS_CKV:ZS_CKV + MLA_KV_RANK], gmkv_ref[...])
            kpe = zs_ref[rows, ZS_KR:ZS_KR + LANES]
            if latent:
                kpe = _rope(kpe, cosc_ref[rows, :], sinc_ref[rows, :], MLA_ROPE // 4)
            else:
                nv_ref[rows, :] = v
                nckv_ref[rows, :] = ckv_n
                nkpe_ref[rows, :] = kpe[:, :MLA_ROPE]
            kpe_ref[prow, :] = kpe.astype(BF16)
            kvm_ref[prow, :] = _dot(ckv_n.astype(BF16), wukv_ref[...]).astype(BF16)
            u_ref[CONV_HALO + c0:CONV_HALO + c0 + qb, :] = (
                zs_ref[rows, ZS_CC:ZS_CC + CONV_CH] * zs_ref[rows, ZS_CH:ZS_CH + CONV_CH])
        for c0 in range(0, seq, qb):
            taps = [u_ref[CONV_HALO - 1 + c0 + i:CONV_HALO - 1 + c0 + i + qb, :] * convw_ref[i:i + 1, :]
                    for i in range(3)]
            conv_ref[c0:c0 + qb, :] = taps[0] + taps[1] + taps[2]

    r0 = pl.multiple_of(j * qb, qb)

    def attend(s, v_bf):
        m = jnp.max(s, axis=-1, keepdims=True)
        e = jnp.exp(s - m)
        return _dot(e.astype(BF16), v_bf) / jnp.sum(e, axis=-1, keepdims=True)

    outs_a = []
    for h in range(GQA_HEADS):
        hk = h // (GQA_HEADS // GQA_KV_HEADS)
        q = _rms(zq_ref[:, h * HEAD_DIM:(h + 1) * HEAD_DIM], gq_ref[...])
        if latent:
            q = _rope(q, cosa_ref[pl.ds(r0, qb), :], sina_ref[pl.ds(r0, qb), :], HEAD_DIM // 4)
        ksl = slice(hk * HEAD_DIM, (hk + 1) * HEAD_DIM)
        s = _dot_nt(q.astype(BF16), kbf_ref[:, ksl]) * (HEAD_DIM ** -0.5)
        outs_a.append(attend(s, vbf_ref[:, ksl]))
    out_a = jnp.concatenate(outs_a, axis=-1)

    cq_n = _rms(zq_ref[:, GQA_WIDTH:GQA_WIDTH + MLA_Q_RANK], gmq_ref[...])
    q_m = _dot(cq_n.astype(BF16), wuq_ref[...])
    outs_c = []
    for h in range(MLA_HEADS):
        q_nope = q_m[:, h * MLA_QH:h * MLA_QH + MLA_NOPE]
        q_pe = q_m[:, h * MLA_QH + MLA_NOPE:(h + 1) * MLA_QH]
        if latent:
            q_pe = _rope(q_pe, cosc_ref[pl.ds(r0, qb), :], sinc_ref[pl.ds(r0, qb), :], MLA_ROPE // 4)
        s = (_dot_nt(q_nope.astype(BF16), kvm_ref[:, h * MLA_KVH:h * MLA_KVH + MLA_NOPE])
             + _dot_nt(q_pe.astype(BF16), kpe_ref[...])) * ((MLA_NOPE + MLA_ROPE) ** -0.5)
        outs_c.append(attend(s, kvm_ref[:, h * MLA_KVH + MLA_NOPE:(h + 1) * MLA_KVH]))
    out_c = jnp.concatenate(outs_c, axis=-1)

    out_b = zq_ref[:, GQA_WIDTH + MLA_Q_RANK:] * conv_ref[pl.ds(r0, qb), :]

    g = ggrp_ref
    merged = jnp.concatenate([
        _rms(out_a, g[:, :GQA_WIDTH]).astype(BF16),
        _rms(out_b, g[:, GQA_WIDTH:GQA_WIDTH + CONV_CH]).astype(BF16),
        _rms(out_c, g[:, GQA_WIDTH + CONV_CH:]).astype(BF16)], axis=-1)
    x_new = x_ref[...] + gt1_ref[...] * _dot(merged, wout_ref[...])
    xo_ref[...] = x_new
    h2 = _rms(x_new, gffn_ref[...]) * (1.0 + sc2_ref[...]) + sh2_ref[...]
    h2_ref[...] = h2.astype(BF16)
    lg_ref[...] = lax.dot_general(wrt_ref[...], h2, (((1,), (1,)), ((), ())),
                                  precision=lax.Precision.HIGHEST, preferred_element_type=F32)


def _mixer(zq, zs, x, mod5, p, layer, *, n_seq, seq, mod_row_fn, cache=None, rope=None, qb=256):
    d = x.shape[1]
    t_out = n_seq * seq
    latent = cache is not None
    past = cache[0].shape[2] if latent else 0
    nq = seq // qb
    sk = past + seq

    def const(shape):
        return _resident(shape, lambda b, j: (0,) * len(shape))

    def lyr(shape):
        return _resident((None,) + shape, lambda b, j: (layer,) + (0,) * len(shape))

    mspec = lambda chunk: pl.BlockSpec((None, None, None, 1, d),
                                       lambda b, j: (layer, mod_row_fn(b), chunk, 0, 0))
    qrow = orow = lambda b, j: (b * nq + j, 0)
    seq_spec = _resident if nq > 1 else pl.BlockSpec
    in_specs = [
        pl.BlockSpec((qb, ZQ_W), qrow),
        seq_spec((seq, ZS_W), lambda b, j: (b, 0)),
        pl.BlockSpec((qb, d), qrow),
        mspec(2), mspec(3), mspec(4),
        lyr((1, HEAD_DIM)), lyr((1, HEAD_DIM)), lyr((1, MLA_Q_RANK)), lyr((1, MLA_KV_RANK)),
        lyr((3, CONV_CH)), lyr((1, d)), lyr((1, d)),
        lyr((MLA_Q_RANK, MLA_HEADS * MLA_QH)), lyr((MLA_KV_RANK, MLA_HEADS * MLA_KVH)),
        lyr((d, d)), lyr((N_EXPERTS, d)),
    ]
    args = [zq, zs, x, mod5, mod5, mod5,
            p["g_q"], p["g_k"], p["g_mla_q"], p["g_mla_kv"], p["conv_w"], p["g_grp"], p["g_ffn"],
            p["w_uq"], p["w_ukv"], p["w_out"], p["w_router_t"]]
    if latent:
        cspec = lambda w: pl.BlockSpec((None, None, past, w), lambda b, j: (b, layer, 0, 0))
        in_specs += [cspec(KV_W), cspec(KV_W), cspec(MLA_KV_RANK), cspec(MLA_ROPE)]
        in_specs += [const((seq, LANES))] * 4
        args += list(cache) + list(rope)
    out_specs = [pl.BlockSpec((qb, d), orow), pl.BlockSpec((qb, d), orow),
                 pl.BlockSpec((N_EXPERTS, qb), lambda b, j: (0, b * nq + j))]
    out_shape = [jax.ShapeDtypeStruct((t_out, d), F32), jax.ShapeDtypeStruct((t_out, d), BF16),
                 jax.ShapeDtypeStruct((N_EXPERTS, t_out), F32)]
    if not latent:
        sspec = lambda w: pl.BlockSpec((None, seq, w), lambda b, j: (b, 0, 0))
        out_specs += [sspec(KV_W), sspec(KV_W), sspec(MLA_KV_RANK), sspec(MLA_ROPE)]
        out_shape += [jax.ShapeDtypeStruct((n_seq, seq, w), F32)
                      for w in (KV_W, KV_W, MLA_KV_RANK, MLA_ROPE)]
    scratch = [pltpu.VMEM((sk, KV_W), BF16), pltpu.VMEM((sk, KV_W), BF16),
               pltpu.VMEM((sk, MLA_HEADS * MLA_KVH), BF16), pltpu.VMEM((sk, LANES), BF16),
               pltpu.VMEM((seq, CONV_CH), F32), pltpu.VMEM((seq + 2 * CONV_HALO, CONV_CH), F32)]
    return pl.pallas_call(
        functools.partial(_mixer_kernel, seq=seq, past=past, qb=qb, latent=latent),
        grid=(n_seq, nq),
        in_specs=in_specs, out_specs=out_specs, out_shape=out_shape, scratch_shapes=scratch,
        compiler_params=_cparams(("parallel", "arbitrary")),
        name="mixer_latent" if latent else "mixer_context",
    )(*args)


def _route(logits, bias):
    tc = logits.shape[1]
    shape3 = (N_GROUPS, GROUP_SIZE, tc)
    scores = jax.nn.sigmoid(logits)
    choice = (scores + bias).reshape(shape3)
    scores = scores.reshape(shape3)
    neg = -jnp.inf
    member = lax.broadcasted_iota(jnp.int32, shape3, 1)
    m1 = jnp.max(choice, axis=1, keepdims=True)
    first = jnp.min(jnp.where(choice == m1, member, GROUP_SIZE), axis=1, keepdims=True)
    m2 = jnp.max(jnp.where(member == first, neg, choice), axis=1, keepdims=True)
    gscore = m1 + m2
    gid = lax.broadcasted_iota(jnp.int32, gscore.shape, 0)
    gmask = jnp.zeros(gscore.shape, jnp.int32)
    for _ in range(TOPK_GROUPS):
        m = jnp.max(gscore, axis=0, keepdims=True)
        pick = jnp.min(jnp.where(gscore == m, gid, N_GROUPS), axis=0, keepdims=True)
        hit = gid == pick
        gmask = jnp.where(hit, 1, gmask)
        gscore = jnp.where(hit, neg, gscore)
    eid = lax.broadcasted_iota(jnp.int32, shape3, 0) * GROUP_SIZE + member
    cur = jnp.where(gmask > 0, choice, neg)
    sel = jnp.zeros(shape3, jnp.int32)
    picks = []
    for _ in range(TOP_K):
        m = jnp.max(jnp.max(cur, axis=0, keepdims=True), axis=1, keepdims=True)
        cand = jnp.where(cur == m, eid, N_EXPERTS)
        pick = jnp.min(jnp.min(cand, axis=0, keepdims=True), axis=1, keepdims=True)
        hit = eid == pick
        sel = jnp.where(hit, 1, sel)
        cur = jnp.where(hit, neg, cur)
        picks.append(pick)
    w = jnp.where(sel > 0, scores, 0.0)
    wsum = jnp.sum(jnp.sum(w, axis=0, keepdims=True), axis=1, keepdims=True)
    return eid, picks, sel, w / wsum * ROUTED_SCALE


def _sum_experts(x3):
    return jnp.sum(jnp.sum(x3, axis=0, keepdims=True), axis=1, keepdims=True)


def _two_part_specs(block, n_first, axis, rest=0):
    def index(part_index):
        def index_map(*grid_ids):
            i = grid_ids[axis]
            idx = [0] * len(block)
            idx[len(block) - 1 - rest] = part_index(i)
            return tuple(idx)
        return index_map
    return [pl.BlockSpec(block, index(lambda i: jnp.minimum(i, n_first - 1))),
            pl.BlockSpec(block, index(lambda i: jnp.maximum(i - n_first, 0)))]


def _router_kernel(lgc_ref, lgl_ref, b_ref, pos_ref, w_ref, cnt_out_ref, cnt_ref, base_ref, *, tm, n_ctx):
    phase, i = pl.program_id(0), pl.program_id(1)
    tc = lgc_ref.shape[1]
    logits = jnp.where(i < n_ctx, lgc_ref[...], lgl_ref[...])
    eid, picks, sel, comb = _route(logits, b_ref[...])
    sel = sel.astype(F32).reshape(N_EXPERTS, tc)

    @pl.when(phase == 0)
    def _count():
        @pl.when(i == 0)
        def _():
            cnt_ref[...] = jnp.zeros_like(cnt_ref)
        cnt_ref[...] += jnp.sum(sel, axis=1, keepdims=True)
        cnt_out_ref[...] = jnp.broadcast_to(cnt_ref[...], cnt_out_ref.shape)
        pos_ref[...] = jnp.zeros_like(pos_ref)
        w_ref[...] = jnp.zeros_like(w_ref)

    @pl.when(phase == 1)
    def _assign():
        @pl.when(i == 0)
        def _():
            tiles = jnp.floor((cnt_ref[...] + (tm - 1)) * (1.0 / tm))
            r = lax.broadcasted_iota(I32, (N_EXPERTS, N_EXPERTS), 0)
            c = lax.broadcasted_iota(I32, (N_EXPERTS, N_EXPERTS), 1)
            lower = (c < r).astype(BF16)
            first_tile = _dot(lower, jnp.broadcast_to(tiles, (N_EXPERTS, LANES)).astype(BF16))
            base_ref[...] = first_tile[:, :1] * tm
        r = lax.broadcasted_iota(I32, (tc, tc), 0)
        c = lax.broadcasted_iota(I32, (tc, tc), 1)
        incl = _dot(sel.astype(BF16), (r <= c).astype(BF16))
        row = (base_ref[...] + incl - 1.0).reshape(eid.shape)
        base_ref[...] += jnp.sum(sel, axis=1, keepdims=True)
        pos, wts = [], []
        for pick in picks:
            hit = eid == pick
            pos.append(_sum_experts(jnp.where(hit, row, 0.0)).reshape(1, tc))
            wts.append(_sum_experts(jnp.where(hit, comb, 0.0)).reshape(1, tc))
        pos_ref[...] = jnp.concatenate(pos, axis=0).astype(I32)
        wts = jnp.concatenate(wts + [jnp.zeros((LANES - TOP_K, tc), F32)], axis=0).T
        w_ref[...] = _rows_to_slab(jnp.concatenate(
            [jnp.broadcast_to(wts[:, k:k + 1], (tc, LANES)) for k in range(TOP_K)], axis=-1))


def _router(logits_ctx, logits_lat, b_router, layer, tm, tc=512):
    e = logits_ctx.shape[0]
    n_ctx = logits_ctx.shape[1] // tc
    t = logits_ctx.shape[1] + logits_lat.shape[1]
    return pl.pallas_call(
        functools.partial(_router_kernel, tm=tm, n_ctx=n_ctx),
        grid=(2, t // tc),
        in_specs=_two_part_specs((e, tc), n_ctx, axis=1) + [
            pl.BlockSpec((None, e, 1), lambda p, i: (layer, 0, 0))],
        out_specs=[pl.BlockSpec((TOP_K, tc), lambda p, i: (0, p * i)),
                   pl.BlockSpec((tc, TOP_K, LANES), lambda p, i: (p * i, 0, 0)),
                   pl.BlockSpec((e, LANES), lambda p, i: (0, 0))],
        out_shape=[jax.ShapeDtypeStruct((TOP_K, t), I32), jax.ShapeDtypeStruct((t, TOP_K, LANES), F32),
                   jax.ShapeDtypeStruct((e, LANES), F32)],
        scratch_shapes=[pltpu.VMEM((e, 1), F32), pltpu.VMEM((e, 1), F32)],
        compiler_params=_cparams(("arbitrary", "arbitrary")),
        name="router",
    )(logits_ctx, logits_lat, b_router.reshape(-1, e, 1))


def _tile_plan(counts, tm, n_tiles):
    tiles = (counts.astype(I32) + (tm - 1)) // tm
    ends = jnp.cumsum(tiles)
    n_active = ends[-1]
    tile = jnp.arange(n_tiles, dtype=I32)
    expert = jnp.sum((tile[:, None] >= ends[None, :]).astype(I32), axis=1)
    expert = jnp.where(tile < n_active, expert, expert[n_active - 1])
    last_tile_row = (ends - 1) * tm
    return expert, n_active.reshape(1), last_tile_row.astype(I32), (tiles > 0).astype(I32)


def _dispatch_kernel(zrow_ref, zflag_ref, hc_ref, hl_ref, pos_ref, xs_ref, h_ref, zero_ref, pos_smem,
                     row_sem, zero_sem, pos_sem, *, tm, n_ctx):
    tt = h_ref.shape[0]
    h = jnp.where(pl.program_id(0) < n_ctx, hc_ref[...], hl_ref[...])
    h_ref[...] = _rows_to_slab(h.astype(F32)).astype(BF16)

    def zero_copy(e):
        return pltpu.make_async_copy(
            zero_ref, xs_ref.at[pl.ds(pl.multiple_of(zrow_ref[e], tm), tm)], zero_sem)

    @pl.when(pl.program_id(0) == 0)
    def _zero_partial_tiles():
        zero_ref[...] = jnp.zeros_like(zero_ref)
        for e in range(N_EXPERTS):
            @pl.when(zflag_ref[e] > 0)
            def _():
                zero_copy(e).start()
        for e in range(N_EXPERTS):
            @pl.when(zflag_ref[e] > 0)
            def _():
                zero_copy(e).wait()

    pos_copy = pltpu.make_async_copy(pos_ref, pos_smem, pos_sem)
    pos_copy.start()
    pos_copy.wait()

    def issue(t, carry):
        for k in range(TOP_K):
            pltpu.make_async_copy(h_ref.at[t], xs_ref.at[pos_smem[k, t]],
                                  row_sem).start(priority=k % N_DMA_PRIORITIES)
        return carry

    lax.fori_loop(0, tt, issue, 0, unroll=8)
    for _ in range(TOP_K):
        pltpu.make_async_copy(h_ref, xs_ref.at[pl.ds(0, tt)], row_sem).wait()


def _dispatch(h_ctx, h_lat, pos, zrow, zflag, n_rows, tm, tt=256):
    d = h_ctx.shape[1]
    n_ctx = h_ctx.shape[0] // tt
    t = h_ctx.shape[0] + h_lat.shape[0]
    return pl.pallas_call(
        functools.partial(_dispatch_kernel, tm=tm, n_ctx=n_ctx),
        grid_spec=pltpu.PrefetchScalarGridSpec(
            num_scalar_prefetch=2,
            grid=(t // tt,),
            in_specs=_two_part_specs((tt, d), n_ctx, axis=0, rest=1) + [
                pl.BlockSpec((TOP_K, tt), lambda i, *_: (0, i))],
            out_specs=pl.BlockSpec(memory_space=pl.ANY),
            scratch_shapes=[pltpu.VMEM((tt, d // LANES, LANES), BF16), pltpu.VMEM((tm, d // LANES, LANES), BF16),
                            pltpu.SMEM((TOP_K, tt), I32),
                            pltpu.SemaphoreType.DMA, pltpu.SemaphoreType.DMA, pltpu.SemaphoreType.DMA],
        ),
        out_shape=jax.ShapeDtypeStruct((n_rows, d // LANES, LANES), BF16),
        compiler_params=_cparams(("arbitrary",)),
        name="moe_dispatch",
    )(zrow, zflag, h_ctx, h_lat, pos)


def _expert_kernel(te_ref, na_ref, xs_ref, wg_ref, wu_ref, wd_ref, o_ref, wg_bf, wu_bf, wd_bf):
    i = pl.program_id(0)

    @pl.when(i < na_ref[0])
    def _():
        @pl.when(jnp.logical_or(i == 0, te_ref[i] != te_ref[jnp.maximum(i - 1, 0)]))
        def _():
            wg_bf[...] = wg_ref[...].astype(BF16)
            wu_bf[...] = wu_ref[...].astype(BF16)
            wd_bf[...] = wd_ref[...].astype(BF16)

        x = _slab_to_rows(xs_ref[...].astype(F32)).astype(BF16)
        hg = _dot(x, wg_bf[...])
        hu = _dot(x, wu_bf[...])
        out = _dot((_silu(hg) * hu).astype(BF16), wd_bf[...])
        o_ref[...] = _rows_to_slab(out).astype(BF16)


def _experts(xs, tile_expert, n_active, w_gate, w_up, w_down, layer, tm):
    n_rows, s, _ = xs.shape
    d, f = w_gate.shape[-2:]
    row = lambda i, te, na: (jnp.minimum(i, na[0] - 1), 0, 0)
    wspec = lambda shape: pl.BlockSpec((None, None) + shape, lambda i, te, na: (layer, te[i], 0, 0))
    return pl.pallas_call(
        _expert_kernel,
        grid_spec=pltpu.PrefetchScalarGridSpec(
            num_scalar_prefetch=2,
            grid=(n_rows // tm,),
            in_specs=[pl.BlockSpec((tm, s, LANES), row), wspec((d, f)), wspec((d, f)), wspec((f, d))],
            out_specs=pl.BlockSpec((tm, s, LANES), row),
            scratch_shapes=[pltpu.VMEM((d, f), BF16), pltpu.VMEM((d, f), BF16), pltpu.VMEM((f, d), BF16)],
        ),
        out_shape=jax.ShapeDtypeStruct((n_rows, s, LANES), BF16),
        compiler_params=_cparams(("arbitrary",)),
        name="moe_experts",
    )(tile_expert, n_active, xs, w_gate, w_up, w_down)


def _combine_kernel(pos_ref, w_ref, h_ref, x_ref, gt2_ref, sg_ref, su_ref, sd_ref, gfin_ref, ys_ref,
                    o_ref, buf_ref, pos_smem, row_sem, pos_sem, *, final):
    tt = x_ref.shape[0]
    pos_copy = pltpu.make_async_copy(pos_ref, pos_smem, pos_sem)
    pos_copy.start()
    pos_copy.wait()

    def issue(t, carry):
        for k in range(TOP_K):
            pltpu.make_async_copy(ys_ref.at[pos_smem[k, t]], buf_ref.at[k, t],
                                  row_sem).start(priority=k % N_DMA_PRIORITIES)
        return carry

    lax.fori_loop(0, tt, issue, 0, unroll=8)
    h = h_ref[...]
    shared = _dot((_silu(_dot(h, sg_ref[...])) * _dot(h, su_ref[...])).astype(BF16), sd_ref[...])
    for k in range(TOP_K):
        pltpu.make_async_copy(ys_ref.at[pl.ds(0, tt)], buf_ref.at[k], row_sem).wait()
    routed = buf_ref[0].astype(F32) * w_ref[:, 0:1, :]
    for k in range(1, TOP_K):
        routed = routed + buf_ref[k].astype(F32) * w_ref[:, k:k + 1, :]
    y = x_ref[...] + gt2_ref[...] * (_slab_to_rows(routed) + shared)
    o_ref[...] = _rms(y, gfin_ref[...]) if final else y


def _combine(pos, w, h2, x, mod5, ys, p, g_final, layer, row_fn, row0, final, tt=COMBINE_TT):
    t, d = x.shape
    f = p["ws_gate"].shape[-1]
    tile0 = row0 // tt
    tok = lambda i: (i, 0)
    lyr = lambda shape: _resident((None,) + shape, lambda i: (layer,) + (0,) * len(shape))
    return pl.pallas_call(
        functools.partial(_combine_kernel, final=final),
        grid=(t // tt,),
        in_specs=[
            pl.BlockSpec((TOP_K, tt), lambda i: (0, tile0 + i)),
            pl.BlockSpec((tt, TOP_K, LANES), lambda i: (tile0 + i, 0, 0)),
            pl.BlockSpec((tt, d), tok), pl.BlockSpec((tt, d), tok),
            pl.BlockSpec((None, None, None, 1, d), lambda i: (layer, row_fn(i), 5, 0, 0)),
            lyr((d, f)), lyr((d, f)), lyr((f, d)),
            _resident((1, d), lambda i: (0, 0)),
            pl.BlockSpec(memory_space=pl.ANY),
        ],
        out_specs=pl.BlockSpec((tt, d), tok),
        out_shape=jax.ShapeDtypeStruct((t, d), F32),
        scratch_shapes=[pltpu.VMEM((TOP_K, tt, d // LANES, LANES), BF16), pltpu.SMEM((TOP_K, tt), I32),
                        pltpu.SemaphoreType.DMA, pltpu.SemaphoreType.DMA],
        compiler_params=_cparams(("arbitrary",)),
        name="moe_combine",
    )(pos, w, h2, x, mod5, p["ws_gate"], p["ws_up"], p["ws_down"], g_final.reshape(1, d), ys)


def _rope_tables(n_tokens, dim):
    rows = n_tokens // GRID_W
    row = jnp.repeat(jnp.arange(rows, dtype=jnp.int32), GRID_W).astype(F32)
    col = jnp.tile(jnp.arange(GRID_W, dtype=jnp.int32), rows).astype(F32)
    half = dim // 2
    inv_freq = ROPE_THETA ** (-(jnp.arange(half // 2, dtype=F32) * 2.0 / half))
    ang_r = row[:, None] * inv_freq[None, :]
    ang_c = col[:, None] * inv_freq[None, :]
    cos = jnp.concatenate([jnp.cos(ang_r)] * 2 + [jnp.cos(ang_c)] * 2, axis=-1)
    sin = jnp.concatenate([-jnp.sin(ang_r), jnp.sin(ang_r), -jnp.sin(ang_c), jnp.sin(ang_c)], axis=-1)
    pad = ((0, 0), (0, LANES - dim))
    return jnp.pad(cos, pad), jnp.pad(sin, pad)


def _prep_params(w_in, w_uq, w_ukv, w_out, w_router, w_gate, w_up, w_down, ws_gate, ws_up, ws_down):
    depth = w_in.shape[0]
    w_in_p = jnp.pad(w_in.astype(BF16), ((0, 0), (0, 0), (0, W_IN_PAD - w_in.shape[-1])))
    w_uq_r = jnp.pad(w_uq.reshape(depth, MLA_Q_RANK, MLA_HEADS, MLA_NOPE + MLA_ROPE),
                     ((0, 0), (0, 0), (0, 0), (0, MLA_QH - MLA_NOPE - MLA_ROPE)))
    w_uq_r = w_uq_r.reshape(depth, MLA_Q_RANK, MLA_HEADS * MLA_QH).astype(BF16)
    return {
        "w_in": w_in_p, "w_uq": w_uq_r, "w_ukv": w_ukv.astype(BF16), "w_out": w_out.astype(BF16),
        "w_router_t": jnp.swapaxes(w_router, 1, 2),
        "w_gate": w_gate, "w_up": w_up, "w_down": w_down,
        "ws_gate": ws_gate.astype(BF16), "ws_up": ws_up.astype(BF16), "ws_down": ws_down.astype(BF16),
    }


def kernel(x_prompt, x_sample, c, cache_gqa_k, cache_gqa_v, cache_mla_ckv, cache_mla_kpe, c_ctx, w_mod, b_mod, g_mix, w_in, g_q, g_k, conv_w, g_mla_q, g_mla_kv, w_uq, w_ukv, g_grp, w_out, g_ffn, w_router, b_router, w_gate, w_up, w_down, ws_gate, ws_up, ws_down, g_final):
    batch, seq, d = x_prompt.shape
    dec_batch, dec_seq, _ = x_sample.shape
    depth = w_mod.shape[0]
    past = cache_gqa_k.shape[2]
    t_ctx, t_lat = batch * seq, dec_batch * dec_seq
    n_moe_tiles = (t_ctx + t_lat) * TOP_K // MOE_TM + N_EXPERTS

    p = _prep_params(w_in, w_uq, w_ukv, w_out, w_router, w_gate, w_up, w_down, ws_gate, ws_up, ws_down)
    for name, val in (("g_q", g_q), ("g_k", g_k), ("g_mla_q", g_mla_q), ("g_mla_kv", g_mla_kv),
                      ("g_grp", g_grp), ("g_ffn", g_ffn)):
        p[name] = val.reshape(depth, 1, -1)
    p["conv_w"] = conv_w

    mod_rows = 8
    cond = jnp.zeros((mod_rows, d), F32).at[0].set(c_ctx).at[1:1 + dec_batch].set(c)
    mod5 = _modulation(cond, w_mod, b_mod).reshape(depth, mod_rows, 6, 1, d)
    ctx_row = lambda i: 0
    lat_row = lambda tile: (lambda i: 1 + i // (dec_seq // tile))

    cache = (cache_gqa_k.reshape(dec_batch, depth, past, KV_W),
             cache_gqa_v.reshape(dec_batch, depth, past, KV_W), cache_mla_ckv, cache_mla_kpe)
    rope = _rope_tables(dec_seq, HEAD_DIM) + _rope_tables(dec_seq, MLA_ROPE)

    xc, xl = x_prompt.reshape(t_ctx, d), x_sample.reshape(t_lat, d)
    new_ctx = []
    for layer in range(depth):
        final = layer == depth - 1
        zqc, zsc = _in_proj(xc, mod5, g_mix, p["w_in"], layer, ctx_row, IN_TM)
        zql, zsl = _in_proj(xl, mod5, g_mix, p["w_in"], layer, lat_row(IN_TM), IN_TM)
        xc, h2c, lgc, nk, nv, nckv, nkpe = _mixer(
            zqc, zsc, xc, mod5, p, layer, n_seq=batch, seq=seq, mod_row_fn=ctx_row)
        xl, h2l, lgl = _mixer(
            zql, zsl, xl, mod5, p, layer, n_seq=dec_batch, seq=dec_seq,
            mod_row_fn=lambda b: 1 + b, cache=cache, rope=rope, qb=128)
        new_ctx.append((nk, nv, nckv, nkpe))
        pos, w_tok, counts = _router(lgc, lgl, b_router, layer, MOE_TM)
        tile_expert, n_active, zrow, zflag = _tile_plan(counts[:, 0], MOE_TM, n_moe_tiles)
        xs = _dispatch(h2c, h2l, pos, zrow, zflag, n_moe_tiles * MOE_TM, MOE_TM)
        ys = _experts(xs, tile_expert, n_active, p["w_gate"], p["w_up"], p["w_down"], layer, MOE_TM)
        xc = _combine(pos, w_tok, h2c, xc, mod5, ys, p, g_final, layer, ctx_row, 0, final)
        xl = _combine(pos, w_tok, h2l, xl, mod5, ys, p, g_final, layer, lat_row(COMBINE_TT), t_ctx, final)
    stack = lambda i: jnp.stack([lc[i] for lc in new_ctx], axis=1)
    new_k = stack(0).reshape(batch, depth, seq, GQA_KV_HEADS, HEAD_DIM)
    new_v = stack(1).reshape(batch, depth, seq, GQA_KV_HEADS, HEAD_DIM)
    return (xc.reshape(batch, seq, d), xl.reshape(dec_batch, dec_seq, d), new_k, new_v, stack(2), stack(3))
```

```python
import functools

import jax
import jax.numpy as jnp
from jax import lax
from jax.experimental import pallas as pl
from jax.experimental.pallas import tpu as pltpu

F32 = jnp.float32
BF16 = jnp.bfloat16
I32 = jnp.int32

EPS = 1e-6
ROPE_THETA = 10000.0
GRID_W = 64
GQA_HEADS = 6
GQA_KV_HEADS = 2
HEAD_DIM = 128
CONV_CH = 512
MLA_HEADS = 6
MLA_Q_RANK = 512
MLA_KV_RANK = 256
MLA_NOPE = 128
MLA_ROPE = 64
MLA_V = 128
GQA_WIDTH = GQA_HEADS * HEAD_DIM
MLA_WIDTH = MLA_HEADS * MLA_V
N_EXPERTS = 64
TOP_K = 8
N_GROUPS = 8
TOPK_GROUPS = 4
GROUP_SIZE = N_EXPERTS // N_GROUPS
ROUTED_SCALE = 2.5
MOE_TM = 256
IN_TM = 256
COMBINE_TT = 128

LANES = 128
SUBLANES = 8
N_DMA_PRIORITIES = 2
VMEM_LIMIT_BYTES = 56 * 1024 * 1024

ZQ_W = GQA_WIDTH + MLA_Q_RANK + CONV_CH
KV_W = GQA_KV_HEADS * HEAD_DIM
ZS_W = 2 * KV_W + MLA_KV_RANK + 2 * CONV_CH + LANES
ZS_K, ZS_V, ZS_CKV = 0, KV_W, 2 * KV_W
ZS_CC = ZS_CKV + MLA_KV_RANK
ZS_CH = ZS_CC + CONV_CH
ZS_KR = ZS_CH + CONV_CH
_IN_Q, _IN_K, _IN_V, _IN_CB, _IN_CC, _IN_CH, _IN_CQ, _IN_CKV, _IN_KR = (
    0, 768, 1024, 1280, 1792, 2304, 2816, 3328, 3584)
W_IN_PAD = _IN_KR + LANES
ZQ_PIECES = ((_IN_Q, GQA_WIDTH), (_IN_CQ, MLA_Q_RANK), (_IN_CB, CONV_CH))
ZS_PIECES = ((_IN_K, 2 * KV_W), (_IN_CKV, MLA_KV_RANK), (_IN_CC, 2 * CONV_CH), (_IN_KR, LANES))
CONV_HALO = 8
MLA_QH = 2 * LANES
MLA_KVH = MLA_NOPE + MLA_V


def _cparams(sem, vmem=VMEM_LIMIT_BYTES):
    return pltpu.CompilerParams(dimension_semantics=sem, vmem_limit_bytes=vmem)


def _resident(shape, index_map):
    return pl.BlockSpec(shape, index_map, pipeline_mode=pl.Buffered(1))


def _rms(x, g):
    ms = jnp.mean(x * x, axis=-1, keepdims=True)
    return x * lax.rsqrt(ms + EPS) * g


def _dot(a, b):
    return jnp.dot(a, b, preferred_element_type=F32)


def _dot_nt(a, b):
    return lax.dot_general(a, b, (((1,), (1,)), ((), ())), preferred_element_type=F32)


def _rows_to_slab(x):
    n, d = x.shape
    s = d // LANES
    y = jnp.stack([x[:, LANES * j:LANES * (j + 1)].reshape(n // SUBLANES, SUBLANES, LANES)
                   for j in range(s)], axis=1)
    return jnp.swapaxes(y, 1, 2).reshape(n, s, LANES)


def _slab_to_rows(x3):
    n, s, _ = x3.shape
    y = jnp.swapaxes(x3.reshape(n // SUBLANES, SUBLANES, s, LANES), 1, 2)
    return jnp.concatenate([y[:, j].reshape(n, LANES) for j in range(s)], axis=-1)


def _silu(x):
    return x * jax.nn.sigmoid(x)


def _rope(x, cos, sin_signed, hb):
    lane = lax.broadcasted_iota(jnp.int32, x.shape, 1)
    partner = jnp.where((lane % (2 * hb)) < hb,
                        pltpu.roll(x, LANES - hb, axis=1), pltpu.roll(x, hb, axis=1))
    return x * cos + partner * sin_signed


def _mod_kernel(c_ref, w_ref, b_ref, o_ref):
    c = c_ref[...]
    a = (c * jax.nn.sigmoid(c)).astype(BF16)
    o_ref[...] = _dot(a, w_ref[...].astype(BF16)) + b_ref[...]


def _modulation(cond, w_mod, b_mod, tn=1024):
    depth, d, n = w_mod.shape
    rows = cond.shape[0]
    return pl.pallas_call(
        _mod_kernel,
        grid=(depth, n // tn),
        in_specs=[
            pl.BlockSpec((rows, d), lambda l, j: (0, 0)),
            pl.BlockSpec((None, d, tn), lambda l, j: (l, 0, j)),
            pl.BlockSpec((None, 1, tn), lambda l, j: (l, 0, j)),
        ],
        out_specs=pl.BlockSpec((None, rows, tn), lambda l, j: (l, 0, j)),
        out_shape=jax.ShapeDtypeStruct((depth, rows, n), F32),
        compiler_params=_cparams(("parallel", "parallel")),
        name="modulation",
    )(cond, w_mod, b_mod.reshape(depth, 1, n))


def _in_proj_kernel(x_ref, g_ref, sh_ref, sc_ref, w_ref, zq_ref, zs_ref):
    h = _rms(x_ref[...], g_ref[...]) * (1.0 + sc_ref[...]) + sh_ref[...]
    z = _dot(h.astype(BF16), w_ref[...])
    for ref, pieces in ((zq_ref, ZQ_PIECES), (zs_ref, ZS_PIECES)):
        dst = 0
        for src, width in pieces:
            ref[:, dst:dst + width] = z[:, src:src + width]
            dst += width


def _in_proj(x, mod5, g_mix, w_in, layer, row_fn, tm):
    t, d = x.shape
    n = w_in.shape[-1]
    mspec = lambda chunk: pl.BlockSpec((None, None, None, 1, d),
                                       lambda i: (layer, row_fn(i), chunk, 0, 0))
    return pl.pallas_call(
        _in_proj_kernel,
        grid=(t // tm,),
        in_specs=[
            pl.BlockSpec((tm, d), lambda i: (i, 0)),
            pl.BlockSpec((None, 1, d), lambda i: (layer, 0, 0)),
            mspec(0), mspec(1),
            _resident((None, d, n), lambda i: (layer, 0, 0)),
        ],
        out_specs=[pl.BlockSpec((tm, ZQ_W), lambda i: (i, 0)),
                   pl.BlockSpec((tm, ZS_W), lambda i: (i, 0))],
        out_shape=[jax.ShapeDtypeStruct((t, ZQ_W), F32), jax.ShapeDtypeStruct((t, ZS_W), F32)],
        compiler_params=_cparams(("parallel",)),
        name="in_proj",
    )(x, g_mix.reshape(-1, 1, d), mod5, mod5, w_in)


def _mixer_kernel(*refs, seq, past, qb, latent):
    it = iter(refs)
    zq_ref, zs_ref, x_ref = next(it), next(it), next(it)
    gt1_ref, sh2_ref, sc2_ref = next(it), next(it), next(it)
    gq_ref, gk_ref, gmq_ref, gmkv_ref = next(it), next(it), next(it), next(it)
    convw_ref, ggrp_ref, gffn_ref = next(it), next(it), next(it)
    wuq_ref, wukv_ref, wout_ref, wrt_ref = next(it), next(it), next(it), next(it)
    if latent:
        ck_ref, cv_ref, cckv_ref, ckpe_ref = next(it), next(it), next(it), next(it)
        cosa_ref, sina_ref, cosc_ref, sinc_ref = next(it), next(it), next(it), next(it)
    xo_ref, h2_ref, lg_ref = next(it), next(it), next(it)
    if not latent:
        nk_ref, nv_ref, nckv_ref, nkpe_ref = next(it), next(it), next(it), next(it)
    kbf_ref, vbf_ref, kvm_ref, kpe_ref, conv_ref, u_ref = (next(it) for _ in range(6))

    j = pl.program_id(1)

    @pl.when(j == 0)
    def _per_sequence():
        if latent:
            kbf_ref[:past, :] = ck_ref[...].astype(BF16)
            vbf_ref[:past, :] = cv_ref[...].astype(BF16)
            kpe_ref[:past, :MLA_ROPE] = ckpe_ref[...].astype(BF16)
            kpe_ref[:past, MLA_ROPE:] = jnp.zeros((past, LANES - MLA_ROPE), BF16)
            for c0 in range(0, past, qb):
                c1 = min(c0 + qb, past)
                kvm_ref[c0:c1, :] = _dot(cckv_ref[c0:c1, :].astype(BF16), wukv_ref[...]).astype(BF16)
        u_ref[:CONV_HALO, :] = jnp.zeros((CONV_HALO, CONV_CH), F32)
        u_ref[CONV_HALO + seq:, :] = jnp.zeros((CONV_HALO, CONV_CH), F32)
        for c0 in range(0, seq, qb):
            rows, prow = slice(c0, c0 + qb), slice(past + c0, past + c0 + qb)
            for hk in range(GQA_KV_HEADS):
                sl = slice(hk * HEAD_DIM, (hk + 1) * HEAD_DIM)
                k = _rms(zs_ref[rows, ZS_K + hk * HEAD_DIM:ZS_K + (hk + 1) * HEAD_DIM], gk_ref[...])
                if latent:
                    k = _rope(k, cosa_ref[rows, :], sina_ref[rows, :], HEAD_DIM // 4)
                else:
                    nk_ref[rows, sl] = k
                kbf_ref[prow, sl] = k.astype(BF16)
            v = zs_ref[rows, ZS_V:ZS_V + KV_W]
            vbf_ref[prow, :] = v.astype(BF16)
            ckv_n = _rms(zs_ref[rows, ZS_CKV:ZS_CKV + MLA_KV_RANK], gmkv_ref[...])
            kpe = zs_ref[rows, ZS_KR:ZS_KR + LANES]
            if latent:
                kpe = _rope(kpe, cosc_ref[rows, :], sinc_ref[rows, :], MLA_ROPE // 4)
            else:
                nv_ref[rows, :] = v
                nckv_ref[rows, :] = ckv_n
                nkpe_ref[rows, :] = kpe[:, :MLA_ROPE]
            kpe_ref[prow, :] = kpe.astype(BF16)
            kvm_ref[prow, :] = _dot(ckv_n.astype(BF16), wukv_ref[...]).astype(BF16)
            u_ref[CONV_HALO + c0:CONV_HALO + c0 + qb, :] = (
                zs_ref[rows, ZS_CC:ZS_CC + CONV_CH] * zs_ref[rows, ZS_CH:ZS_CH + CONV_CH])
        for c0 in range(0, seq, qb):
            taps = [u_ref[CONV_HALO - 1 + c0 + i:CONV_HALO - 1 + c0 + i + qb, :] * convw_ref[i:i + 1, :]
                    for i in range(3)]
            conv_ref[c0:c0 + qb, :] = taps[0] + taps[1] + taps[2]

    r0 = pl.multiple_of(j * qb, qb)

    def attend(s, v_bf):
        m = jnp.max(s, axis=-1, keepdims=True)
        e = jnp.exp(s - m)
        return _dot(e.astype(BF16), v_bf) / jnp.sum(e, axis=-1, keepdims=True)

    outs_a = []
    for h in range(GQA_HEADS):
        hk = h // (GQA_HEADS // GQA_KV_HEADS)
        q = _rms(zq_ref[:, h * HEAD_DIM:(h + 1) * HEAD_DIM], gq_ref[...])
        if latent:
            q = _rope(q, cosa_ref[pl.ds(r0, qb), :], sina_ref[pl.ds(r0, qb), :], HEAD_DIM // 4)
        ksl = slice(hk * HEAD_DIM, (hk + 1) * HEAD_DIM)
        s = _dot_nt(q.astype(BF16), kbf_ref[:, ksl]) * (HEAD_DIM ** -0.5)
        outs_a.append(attend(s, vbf_ref[:, ksl]))
    out_a = jnp.concatenate(outs_a, axis=-1)

    cq_n = _rms(zq_ref[:, GQA_WIDTH:GQA_WIDTH + MLA_Q_RANK], gmq_ref[...])
    q_m = _dot(cq_n.astype(BF16), wuq_ref[...])
    outs_c = []
    for h in range(MLA_HEADS):
        q_nope = q_m[:, h * MLA_QH:h * MLA_QH + MLA_NOPE]
        q_pe = q_m[:, h * MLA_QH + MLA_NOPE:(h + 1) * MLA_QH]
        if latent:
            q_pe = _rope(q_pe, cosc_ref[pl.ds(r0, qb), :], sinc_ref[pl.ds(r0, qb), :], MLA_ROPE // 4)
        s = (_dot_nt(q_nope.astype(BF16), kvm_ref[:, h * MLA_KVH:h * MLA_KVH + MLA_NOPE])
             + _dot_nt(q_pe.astype(BF16), kpe_ref[...])) * ((MLA_NOPE + MLA_ROPE) ** -0.5)
        outs_c.append(attend(s, kvm_ref[:, h * MLA_KVH + MLA_NOPE:(h + 1) * MLA_KVH]))
    out_c = jnp.concatenate(outs_c, axis=-1)

    out_b = zq_ref[:, GQA_WIDTH + MLA_Q_RANK:] * conv_ref[pl.ds(r0, qb), :]

    g = ggrp_ref
    merged = jnp.concatenate([
        _rms(out_a, g[:, :GQA_WIDTH]).astype(BF16),
        _rms(out_b, g[:, GQA_WIDTH:GQA_WIDTH + CONV_CH]).astype(BF16),
        _rms(out_c, g[:, GQA_WIDTH + CONV_CH:]).astype(BF16)], axis=-1)
    x_new = x_ref[...] + gt1_ref[...] * _dot(merged, wout_ref[...])
    xo_ref[...] = x_new
    h2 = _rms(x_new, gffn_ref[...]) * (1.0 + sc2_ref[...]) + sh2_ref[...]
    h2_ref[...] = h2.astype(BF16)
    lg_ref[...] = lax.dot_general(wrt_ref[...], h2, (((1,), (1,)), ((), ())),
                                  precision=lax.Precision.HIGHEST, preferred_element_type=F32)


def _mixer(zq, zs, x, mod5, p, layer, *, n_seq, seq, mod_row_fn, cache=None, rope=None, qb=256):
    d = x.shape[1]
    t_out = n_seq * seq
    latent = cache is not None
    past = cache[0].shape[2] if latent else 0
    nq = seq // qb
    sk = past + seq

    def const(shape):
        return _resident(shape, lambda b, j: (0,) * len(shape))

    def lyr(shape):
        return _resident((None,) + shape, lambda b, j: (layer,) + (0,) * len(shape))

    mspec = lambda chunk: pl.BlockSpec((None, None, None, 1, d),
                                       lambda b, j: (layer, mod_row_fn(b), chunk, 0, 0))
    qrow = orow = lambda b, j: (b * nq + j, 0)
    seq_spec = _resident if nq > 1 else pl.BlockSpec
    in_specs = [
        pl.BlockSpec((qb, ZQ_W), qrow),
        seq_spec((seq, ZS_W), lambda b, j: (b, 0)),
        pl.BlockSpec((qb, d), qrow),
        mspec(2), mspec(3), mspec(4),
        lyr((1, HEAD_DIM)), lyr((1, HEAD_DIM)), lyr((1, MLA_Q_RANK)), lyr((1, MLA_KV_RANK)),
        lyr((3, CONV_CH)), lyr((1, d)), lyr((1, d)),
        lyr((MLA_Q_RANK, MLA_HEADS * MLA_QH)), lyr((MLA_KV_RANK, MLA_HEADS * MLA_KVH)),
        lyr((d, d)), lyr((N_EXPERTS, d)),
    ]
    args = [zq, zs, x, mod5, mod5, mod5,
            p["g_q"], p["g_k"], p["g_mla_q"], p["g_mla_kv"], p["conv_w"], p["g_grp"], p["g_ffn"],
            p["w_uq"], p["w_ukv"], p["w_out"], p["w_router_t"]]
    if latent:
        cspec = lambda w: pl.BlockSpec((None, None, past, w), lambda b, j: (b, layer, 0, 0))
        in_specs += [cspec(KV_W), cspec(KV_W), cspec(MLA_KV_RANK), cspec(MLA_ROPE)]
        in_specs += [const((seq, LANES))] * 4
        args += list(cache) + list(rope)
    out_specs = [pl.BlockSpec((qb, d), orow), pl.BlockSpec((qb, d), orow),
                 pl.BlockSpec((N_EXPERTS, qb), lambda b, j: (0, b * nq + j))]
    out_shape = [jax.ShapeDtypeStruct((t_out, d), F32), jax.ShapeDtypeStruct((t_out, d), BF16),
                 jax.ShapeDtypeStruct((N_EXPERTS, t_out), F32)]
    if not latent:
        sspec = lambda w: pl.BlockSpec((None, seq, w), lambda b, j: (b, 0, 0))
        out_specs += [sspec(KV_W), sspec(KV_W), sspec(MLA_KV_RANK), sspec(MLA_ROPE)]
        out_shape += [jax.ShapeDtypeStruct((n_seq, seq, w), F32)
                      for w in (KV_W, KV_W, MLA_KV_RANK, MLA_ROPE)]
    scratch = [pltpu.VMEM((sk, KV_W), BF16), pltpu.VMEM((sk, KV_W), BF16),
               pltpu.VMEM((sk, MLA_HEADS * MLA_KVH), BF16), pltpu.VMEM((sk, LANES), BF16),
               pltpu.VMEM((seq, CONV_CH), F32), pltpu.VMEM((seq + 2 * CONV_HALO, CONV_CH), F32)]
    return pl.pallas_call(
        functools.partial(_mixer_kernel, seq=seq, past=past, qb=qb, latent=latent),
        grid=(n_seq, nq),
        in_specs=in_specs, out_specs=out_specs, out_shape=out_shape, scratch_shapes=scratch,
        compiler_params=_cparams(("parallel", "arbitrary")),
        name="mixer_latent" if latent else "mixer_context",
    )(*args)


def _route(logits, bias):
    tc = logits.shape[1]
    shape3 = (N_GROUPS, GROUP_SIZE, tc)
    scores = jax.nn.sigmoid(logits)
    choice = (scores + bias).reshape(shape3)
    scores = scores.reshape(shape3)
    neg = -jnp.inf
    member = lax.broadcasted_iota(jnp.int32, shape3, 1)
    m1 = jnp.max(choice, axis=1, keepdims=True)
    first = jnp.min(jnp.where(choice == m1, member, GROUP_SIZE), axis=1, keepdims=True)
    m2 = jnp.max(jnp.where(member == first, neg, choice), axis=1, keepdims=True)
    gscore = m1 + m2
    gid = lax.broadcasted_iota(jnp.int32, gscore.shape, 0)
    gmask = jnp.zeros(gscore.shape, jnp.int32)
    for _ in range(TOPK_GROUPS):
        m = jnp.max(gscore, axis=0, keepdims=True)
        pick = jnp.min(jnp.where(gscore == m, gid, N_GROUPS), axis=0, keepdims=True)
        hit = gid == pick
        gmask = jnp.where(hit, 1, gmask)
        gscore = jnp.where(hit, neg, gscore)
    eid = lax.broadcasted_iota(jnp.int32, shape3, 0) * GROUP_SIZE + member
    cur = jnp.where(gmask > 0, choice, neg)
    sel = jnp.zeros(shape3, jnp.int32)
    picks = []
    for _ in range(TOP_K):
        m = jnp.max(jnp.max(cur, axis=0, keepdims=True), axis=1, keepdims=True)
        cand = jnp.where(cur == m, eid, N_EXPERTS)
        pick = jnp.min(jnp.min(cand, axis=0, keepdims=True), axis=1, keepdims=True)
        hit = eid == pick
        sel = jnp.where(hit, 1, sel)
        cur = jnp.where(hit, neg, cur)
        picks.append(pick)
    w = jnp.where(sel > 0, scores, 0.0)
    wsum = jnp.sum(jnp.sum(w, axis=0, keepdims=True), axis=1, keepdims=True)
    return eid, picks, sel, w / wsum * ROUTED_SCALE


def _sum_experts(x3):
    return jnp.sum(jnp.sum(x3, axis=0, keepdims=True), axis=1, keepdims=True)


def _two_part_specs(block, n_first, axis, rest=0):
    def index(part_index):
        def index_map(*grid_ids):
            i = grid_ids[axis]
            idx = [0] * len(block)
            idx[len(block) - 1 - rest] = part_index(i)
            return tuple(idx)
        return index_map
    return [pl.BlockSpec(block, index(lambda i: jnp.minimum(i, n_first - 1))),
            pl.BlockSpec(block, index(lambda i: jnp.maximum(i - n_first, 0)))]


def _router_kernel(lgc_ref, lgl_ref, b_ref, pos_ref, w_ref, cnt_out_ref, cnt_ref, base_ref, *, tm, n_ctx):
    phase, i = pl.program_id(0), pl.program_id(1)
    tc = lgc_ref.shape[1]
    logits = jnp.where(i < n_ctx, lgc_ref[...], lgl_ref[...])
    eid, picks, sel, comb = _route(logits, b_ref[...])
    sel = sel.astype(F32).reshape(N_EXPERTS, tc)

    @pl.when(phase == 0)
    def _count():
        @pl.when(i == 0)
        def _():
            cnt_ref[...] = jnp.zeros_like(cnt_ref)
        cnt_ref[...] += jnp.sum(sel, axis=1, keepdims=True)
        cnt_out_ref[...] = jnp.broadcast_to(cnt_ref[...], cnt_out_ref.shape)
        pos_ref[...] = jnp.zeros_like(pos_ref)
        w_ref[...] = jnp.zeros_like(w_ref)

    @pl.when(phase == 1)
    def _assign():
        @pl.when(i == 0)
        def _():
            tiles = jnp.floor((cnt_ref[...] + (tm - 1)) * (1.0 / tm))
            r = lax.broadcasted_iota(I32, (N_EXPERTS, N_EXPERTS), 0)
            c = lax.broadcasted_iota(I32, (N_EXPERTS, N_EXPERTS), 1)
            lower = (c < r).astype(BF16)
            first_tile = _dot(lower, jnp.broadcast_to(tiles, (N_EXPERTS, LANES)).astype(BF16))
            base_ref[...] = first_tile[:, :1] * tm
        r = lax.broadcasted_iota(I32, (tc, tc), 0)
        c = lax.broadcasted_iota(I32, (tc, tc), 1)
        incl = _dot(sel.astype(BF16), (r <= c).astype(BF16))
        row = (base_ref[...] + incl - 1.0).reshape(eid.shape)
        base_ref[...] += jnp.sum(sel, axis=1, keepdims=True)
        pos, wts = [], []
        for pick in picks:
            hit = eid == pick
            pos.append(_sum_experts(jnp.where(hit, row, 0.0)).reshape(1, tc))
            wts.append(_sum_experts(jnp.where(hit, comb, 0.0)).reshape(1, tc))
        pos_ref[...] = jnp.concatenate(pos, axis=0).astype(I32)
        wts = jnp.concatenate(wts + [jnp.zeros((LANES - TOP_K, tc), F32)], axis=0).T
        w_ref[...] = _rows_to_slab(jnp.concatenate(
            [jnp.broadcast_to(wts[:, k:k + 1], (tc, LANES)) for k in range(TOP_K)], axis=-1))


def _router(logits_ctx, logits_lat, b_router, layer, tm, tc=512):
    e = logits_ctx.shape[0]
    n_ctx = logits_ctx.shape[1] // tc
    t = logits_ctx.shape[1] + logits_lat.shape[1]
    return pl.pallas_call(
        functools.partial(_router_kernel, tm=tm, n_ctx=n_ctx),
        grid=(2, t // tc),
        in_specs=_two_part_specs((e, tc), n_ctx, axis=1) + [
            pl.BlockSpec((None, e, 1), lambda p, i: (layer, 0, 0))],
        out_specs=[pl.BlockSpec((TOP_K, tc), lambda p, i: (0, p * i)),
                   pl.BlockSpec((tc, TOP_K, LANES), lambda p, i: (p * i, 0, 0)),
                   pl.BlockSpec((e, LANES), lambda p, i: (0, 0))],
        out_shape=[jax.ShapeDtypeStruct((TOP_K, t), I32), jax.ShapeDtypeStruct((t, TOP_K, LANES), F32),
                   jax.ShapeDtypeStruct((e, LANES), F32)],
        scratch_shapes=[pltpu.VMEM((e, 1), F32), pltpu.VMEM((e, 1), F32)],
        compiler_params=_cparams(("arbitrary", "arbitrary")),
        name="router",
    )(logits_ctx, logits_lat, b_router.reshape(-1, e, 1))


def _tile_plan(counts, tm):
    tiles = (counts.astype(I32) + (tm - 1)) // tm
    ends = jnp.cumsum(tiles)
    return (ends - tiles) * tm, tiles


def _dispatch_kernel(row0_ref, tiles_ref, hc_ref, hl_ref, pos_ref, xs_ref, h_ref, zero_ref, pos_smem,
                     row_sem, zero_sem, pos_sem, *, tm, n_ctx):
    tt = h_ref.shape[0]
    h = jnp.where(pl.program_id(0) < n_ctx, hc_ref[...], hl_ref[...])
    h_ref[...] = _rows_to_slab(h.astype(F32)).astype(BF16)

    def zero_copy(e):
        row = row0_ref[e] + (tiles_ref[e] - 1) * tm
        return pltpu.make_async_copy(zero_ref, xs_ref.at[pl.ds(pl.multiple_of(row, tm), tm)], zero_sem)

    @pl.when(pl.program_id(0) == 0)
    def _zero_partial_tiles():
        zero_ref[...] = jnp.zeros_like(zero_ref)
        for e in range(N_EXPERTS):
            @pl.when(tiles_ref[e] > 0)
            def _():
                zero_copy(e).start()
        for e in range(N_EXPERTS):
            @pl.when(tiles_ref[e] > 0)
            def _():
                zero_copy(e).wait()

    pos_copy = pltpu.make_async_copy(pos_ref, pos_smem, pos_sem)
    pos_copy.start()
    pos_copy.wait()

    def issue(t, carry):
        for k in range(TOP_K):
            pltpu.make_async_copy(h_ref.at[t], xs_ref.at[pos_smem[k, t]],
                                  row_sem).start(priority=k % N_DMA_PRIORITIES)
        return carry

    lax.fori_loop(0, tt, issue, 0, unroll=8)
    for _ in range(TOP_K):
        pltpu.make_async_copy(h_ref, xs_ref.at[pl.ds(0, tt)], row_sem).wait()


def _dispatch(h_ctx, h_lat, pos, first_row, tiles, n_rows, tm, tt=256):
    d = h_ctx.shape[1]
    n_ctx = h_ctx.shape[0] // tt
    t = h_ctx.shape[0] + h_lat.shape[0]
    return pl.pallas_call(
        functools.partial(_dispatch_kernel, tm=tm, n_ctx=n_ctx),
        grid_spec=pltpu.PrefetchScalarGridSpec(
            num_scalar_prefetch=2,
            grid=(t // tt,),
            in_specs=_two_part_specs((tt, d), n_ctx, axis=0, rest=1) + [
                pl.BlockSpec((TOP_K, tt), lambda i, *_: (0, i))],
            out_specs=pl.BlockSpec(memory_space=pl.ANY),
            scratch_shapes=[pltpu.VMEM((tt, d // LANES, LANES), BF16), pltpu.VMEM((tm, d // LANES, LANES), BF16),
                            pltpu.SMEM((TOP_K, tt), I32),
                            pltpu.SemaphoreType.DMA, pltpu.SemaphoreType.DMA, pltpu.SemaphoreType.DMA],
        ),
        out_shape=jax.ShapeDtypeStruct((n_rows, d // LANES, LANES), BF16),
        compiler_params=_cparams(("arbitrary",)),
        name="moe_dispatch",
    )(first_row, tiles, h_ctx, h_lat, pos)


def _expert_kernel(row0_ref, tiles_ref, wg_ref, wu_ref, wd_ref, xs_ref, ys_ref,
                   xbuf, obuf, wg_bf, wu_bf, wd_bf, in_sem, out_sem, *, tm):
    e = pl.program_id(0)
    n_tiles = tiles_ref[e]

    def rows(j):
        return pl.ds(pl.multiple_of(row0_ref[e] + j * tm, tm), tm)

    def fetch(j, slot):
        return pltpu.make_async_copy(xs_ref.at[rows(j)], xbuf.at[slot], in_sem.at[slot])

    def store(j, slot):
        return pltpu.make_async_copy(obuf.at[slot], ys_ref.at[rows(j)], out_sem.at[slot])

    @pl.when(n_tiles > 0)
    def _():
        fetch(0, 0).start()
        wg_bf[...] = wg_ref[...].astype(BF16)
        wu_bf[...] = wu_ref[...].astype(BF16)
        wd_bf[...] = wd_ref[...].astype(BF16)

        def tile(j, carry):
            slot = j % 2

            @pl.when(j + 1 < n_tiles)
            def _():
                fetch(j + 1, 1 - slot).start()

            fetch(j, slot).wait()

            @pl.when(j >= 2)
            def _():
                store(j - 2, slot).wait()

            x = _slab_to_rows(xbuf[slot].astype(F32)).astype(BF16)
            hg = _dot(x, wg_bf[...])
            hu = _dot(x, wu_bf[...])
            out = _dot((_silu(hg) * hu).astype(BF16), wd_bf[...])
            obuf[slot] = _rows_to_slab(out).astype(BF16)
            store(j, slot).start()
            return carry

        lax.fori_loop(0, n_tiles, tile, 0)

        @pl.when(n_tiles >= 2)
        def _():
            store(n_tiles - 2, n_tiles % 2).wait()

        store(n_tiles - 1, (n_tiles - 1) % 2).wait()


def _experts(xs, first_row, tiles, w_gate, w_up, w_down, layer, tm):
    n_rows, s, _ = xs.shape
    d, f = w_gate.shape[-2:]
    wspec = lambda shape: pl.BlockSpec((None, None) + shape, lambda e, *_: (layer, e, 0, 0))
    return pl.pallas_call(
        functools.partial(_expert_kernel, tm=tm),
        grid_spec=pltpu.PrefetchScalarGridSpec(
            num_scalar_prefetch=2,
            grid=(N_EXPERTS,),
            in_specs=[wspec((d, f)), wspec((d, f)), wspec((f, d)), pl.BlockSpec(memory_space=pl.ANY)],
            out_specs=pl.BlockSpec(memory_space=pl.ANY),
            scratch_shapes=[pltpu.VMEM((2, tm, s, LANES), BF16), pltpu.VMEM((2, tm, s, LANES), BF16),
                            pltpu.VMEM((d, f), BF16), pltpu.VMEM((d, f), BF16), pltpu.VMEM((f, d), BF16),
                            pltpu.SemaphoreType.DMA((2,)), pltpu.SemaphoreType.DMA((2,))],
        ),
        out_shape=jax.ShapeDtypeStruct((n_rows, s, LANES), BF16),
        compiler_params=_cparams(("arbitrary",)),
        name="moe_experts",
    )(first_row, tiles, w_gate, w_up, w_down, xs)


def _combine_kernel(pos_ref, posn_ref, w_ref, h_ref, x_ref, gt2_ref, sg_ref, su_ref, sd_ref, gfin_ref,
                    ys_ref, o_ref, buf_ref, pos_smem, row_sem, pos_sem, *, final):
    i, n = pl.program_id(0), pl.num_programs(0)
    tt = x_ref.shape[0]

    def gather(tile_pos_ref, slot):
        pos_copy = pltpu.make_async_copy(tile_pos_ref, pos_smem.at[slot], pos_sem)
        pos_copy.start()
        pos_copy.wait()

        def issue(t, carry):
            for k in range(TOP_K):
                pltpu.make_async_copy(ys_ref.at[pos_smem[slot, k, t]], buf_ref.at[slot, k, t],
                                      row_sem.at[slot]).start(priority=k % N_DMA_PRIORITIES)
            return carry

        lax.fori_loop(0, tt, issue, 0, unroll=8)

    @pl.when(i == 0)
    def _():
        gather(pos_ref, 0)

    @pl.when(i + 1 < n)
    def _():
        gather(posn_ref, (i + 1) % 2)

    slot = i % 2
    h = h_ref[...]
    shared = _dot((_silu(_dot(h, sg_ref[...])) * _dot(h, su_ref[...])).astype(BF16), sd_ref[...])
    for k in range(TOP_K):
        pltpu.make_async_copy(ys_ref.at[pl.ds(0, tt)], buf_ref.at[slot, k], row_sem.at[slot]).wait()
    routed = buf_ref[slot, 0].astype(F32) * w_ref[:, 0:1, :]
    for k in range(1, TOP_K):
        routed = routed + buf_ref[slot, k].astype(F32) * w_ref[:, k:k + 1, :]
    y = x_ref[...] + gt2_ref[...] * (_slab_to_rows(routed) + shared)
    o_ref[...] = _rms(y, gfin_ref[...]) if final else y


def _combine(pos, w, h2, x, mod5, ys, p, g_final, layer, row_fn, row0, final, tt=COMBINE_TT):
    t, d = x.shape
    f = p["ws_gate"].shape[-1]
    tile0, n = row0 // tt, t // tt
    tok = lambda i: (i, 0)
    lyr = lambda shape: _resident((None,) + shape, lambda i: (layer,) + (0,) * len(shape))
    return pl.pallas_call(
        functools.partial(_combine_kernel, final=final),
        grid=(n,),
        in_specs=[
            pl.BlockSpec((TOP_K, tt), lambda i: (0, tile0 + i)),
            pl.BlockSpec((TOP_K, tt), lambda i: (0, tile0 + jnp.minimum(i + 1, n - 1))),
            pl.BlockSpec((tt, TOP_K, LANES), lambda i: (tile0 + i, 0, 0)),
            pl.BlockSpec((tt, d), tok), pl.BlockSpec((tt, d), tok),
            pl.BlockSpec((None, None, None, 1, d), lambda i: (layer, row_fn(i), 5, 0, 0)),
            lyr((d, f)), lyr((d, f)), lyr((f, d)),
            _resident((1, d), lambda i: (0, 0)),
            pl.BlockSpec(memory_space=pl.ANY),
        ],
        out_specs=pl.BlockSpec((tt, d), tok),
        out_shape=jax.ShapeDtypeStruct((t, d), F32),
        scratch_shapes=[pltpu.VMEM((2, TOP_K, tt, d // LANES, LANES), BF16), pltpu.SMEM((2, TOP_K, tt), I32),
                        pltpu.SemaphoreType.DMA((2,)), pltpu.SemaphoreType.DMA],
        compiler_params=_cparams(("arbitrary",)),
        name="moe_combine",
    )(pos, pos, w, h2, x, mod5, p["ws_gate"], p["ws_up"], p["ws_down"], g_final.reshape(1, d), ys)


def _rope_tables(n_tokens, dim):
    rows = n_tokens // GRID_W
    row = jnp.repeat(jnp.arange(rows, dtype=jnp.int32), GRID_W).astype(F32)
    col = jnp.tile(jnp.arange(GRID_W, dtype=jnp.int32), rows).astype(F32)
    half = dim // 2
    inv_freq = ROPE_THETA ** (-(jnp.arange(half // 2, dtype=F32) * 2.0 / half))
    ang_r = row[:, None] * inv_freq[None, :]
    ang_c = col[:, None] * inv_freq[None, :]
    cos = jnp.concatenate([jnp.cos(ang_r)] * 2 + [jnp.cos(ang_c)] * 2, axis=-1)
    sin = jnp.concatenate([-jnp.sin(ang_r), jnp.sin(ang_r), -jnp.sin(ang_c), jnp.sin(ang_c)], axis=-1)
    pad = ((0, 0), (0, LANES - dim))
    return jnp.pad(cos, pad), jnp.pad(sin, pad)


def _prep_params(w_in, w_uq, w_ukv, w_out, w_router, w_gate, w_up, w_down, ws_gate, ws_up, ws_down):
    depth = w_in.shape[0]
    w_in_p = jnp.pad(w_in.astype(BF16), ((0, 0), (0, 0), (0, W_IN_PAD - w_in.shape[-1])))
    w_uq_r = jnp.pad(w_uq.reshape(depth, MLA_Q_RANK, MLA_HEADS, MLA_NOPE + MLA_ROPE),
                     ((0, 0), (0, 0), (0, 0), (0, MLA_QH - MLA_NOPE - MLA_ROPE)))
    w_uq_r = w_uq_r.reshape(depth, MLA_Q_RANK, MLA_HEADS * MLA_QH).astype(BF16)
    return {
        "w_in": w_in_p, "w_uq": w_uq_r, "w_ukv": w_ukv.astype(BF16), "w_out": w_out.astype(BF16),
        "w_router_t": jnp.swapaxes(w_router, 1, 2),
        "w_gate": w_gate, "w_up": w_up, "w_down": w_down,
        "ws_gate": ws_gate.astype(BF16), "ws_up": ws_up.astype(BF16), "ws_down": ws_down.astype(BF16),
    }


def kernel(x_prompt, x_sample, c, cache_gqa_k, cache_gqa_v, cache_mla_ckv, cache_mla_kpe, c_ctx, w_mod, b_mod, g_mix, w_in, g_q, g_k, conv_w, g_mla_q, g_mla_kv, w_uq, w_ukv, g_grp, w_out, g_ffn, w_router, b_router, w_gate, w_up, w_down, ws_gate, ws_up, ws_down, g_final):
    batch, seq, d = x_prompt.shape
    dec_batch, dec_seq, _ = x_sample.shape
    depth = w_mod.shape[0]
    past = cache_gqa_k.shape[2]
    t_ctx, t_lat = batch * seq, dec_batch * dec_seq
    n_moe_tiles = (t_ctx + t_lat) * TOP_K // MOE_TM + N_EXPERTS

    p = _prep_params(w_in, w_uq, w_ukv, w_out, w_router, w_gate, w_up, w_down, ws_gate, ws_up, ws_down)
    for name, val in (("g_q", g_q), ("g_k", g_k), ("g_mla_q", g_mla_q), ("g_mla_kv", g_mla_kv),
                      ("g_grp", g_grp), ("g_ffn", g_ffn)):
        p[name] = val.reshape(depth, 1, -1)
    p["conv_w"] = conv_w

    mod_rows = 8
    cond = jnp.zeros((mod_rows, d), F32).at[0].set(c_ctx).at[1:1 + dec_batch].set(c)
    mod5 = _modulation(cond, w_mod, b_mod).reshape(depth, mod_rows, 6, 1, d)
    ctx_row = lambda i: 0
    lat_row = lambda tile: (lambda i: 1 + i // (dec_seq // tile))

    cache = (cache_gqa_k.reshape(dec_batch, depth, past, KV_W),
             cache_gqa_v.reshape(dec_batch, depth, past, KV_W), cache_mla_ckv, cache_mla_kpe)
    rope = _rope_tables(dec_seq, HEAD_DIM) + _rope_tables(dec_seq, MLA_ROPE)

    xc, xl = x_prompt.reshape(t_ctx, d), x_sample.reshape(t_lat, d)
    new_ctx = []
    for layer in range(depth):
        final = layer == depth - 1
        zqc, zsc = _in_proj(xc, mod5, g_mix, p["w_in"], layer, ctx_row, IN_TM)
        zql, zsl = _in_proj(xl, mod5, g_mix, p["w_in"], layer, lat_row(IN_TM), IN_TM)
        xc, h2c, lgc, nk, nv, nckv, nkpe = _mixer(
            zqc, zsc, xc, mod5, p, layer, n_seq=batch, seq=seq, mod_row_fn=ctx_row)
        xl, h2l, lgl = _mixer(
            zql, zsl, xl, mod5, p, layer, n_seq=dec_batch, seq=dec_seq,
            mod_row_fn=lambda b: 1 + b, cache=cache, rope=rope, qb=128)
        new_ctx.append((nk, nv, nckv, nkpe))
        pos, w_tok, counts = _router(lgc, lgl, b_router, layer, MOE_TM)
        first_row, tiles = _tile_plan(counts[:, 0], MOE_TM)
        xs = _dispatch(h2c, h2l, pos, first_row, tiles, n_moe_tiles * MOE_TM, MOE_TM)
        ys = _experts(xs, first_row, tiles, p["w_gate"], p["w_up"], p["w_down"], layer, MOE_TM)
        xc = _combine(pos, w_tok, h2c, xc, mod5, ys, p, g_final, layer, ctx_row, 0, final)
        xl = _combine(pos, w_tok, h2l, xl, mod5, ys, p, g_final, layer, lat_row(COMBINE_TT), t_ctx, final)
    stack = lambda i: jnp.stack([lc[i] for lc in new_ctx], axis=1)
    new_k = stack(0).reshape(batch, depth, seq, GQA_KV_HEADS, HEAD_DIM)
    new_v = stack(1).reshape(batch, depth, seq, GQA_KV_HEADS, HEAD_DIM)
    return (xc.reshape(batch, seq, d), xl.reshape(dec_batch, dec_seq, d), new_k, new_v, stack(2), stack(3))
```

```python
import functools

import jax
import jax.numpy as jnp
from jax import lax
from jax.experimental import pallas as pl
from jax.experimental.pallas import tpu as pltpu

F32 = jnp.float32
BF16 = jnp.bfloat16
I32 = jnp.int32

EPS = 1e-6
ROPE_THETA = 10000.0
GRID_W = 64
GQA_HEADS = 6
GQA_KV_HEADS = 2
HEAD_DIM = 128
CONV_CH = 512
MLA_HEADS = 6
MLA_Q_RANK = 512
MLA_KV_RANK = 256
MLA_NOPE = 128
MLA_ROPE = 64
MLA_V = 128
GQA_WIDTH = GQA_HEADS * HEAD_DIM
MLA_WIDTH = MLA_HEADS * MLA_V
N_EXPERTS = 64
TOP_K = 8
N_GROUPS = 8
TOPK_GROUPS = 4
GROUP_SIZE = N_EXPERTS // N_GROUPS
ROUTED_SCALE = 2.5
MOE_TM = 512
IN_TM = 512
COMBINE_TT = 256
DISPATCH_TT = 512
LANES = 128
SUBLANES = 8
N_DMA_PRIORITIES = 2
VMEM_LIMIT_BYTES = 56 * 1024 * 1024

ZQ_W = GQA_WIDTH + MLA_Q_RANK + CONV_CH
KV_W = GQA_KV_HEADS * HEAD_DIM
ZS_W = 2 * KV_W + MLA_KV_RANK + 2 * CONV_CH + LANES
ZS_K, ZS_V, ZS_CKV = 0, KV_W, 2 * KV_W
ZS_CC = ZS_CKV + MLA_KV_RANK
ZS_CH = ZS_CC + CONV_CH
ZS_KR = ZS_CH + CONV_CH
_IN_Q, _IN_K, _IN_V, _IN_CB, _IN_CC, _IN_CH, _IN_CQ, _IN_CKV, _IN_KR = (
    0, 768, 1024, 1280, 1792, 2304, 2816, 3328, 3584)
W_IN_PAD = _IN_KR + LANES
ZQ_PIECES = ((_IN_Q, GQA_WIDTH), (_IN_CQ, MLA_Q_RANK), (_IN_CB, CONV_CH))
ZS_PIECES = ((_IN_K, 2 * KV_W), (_IN_CKV, MLA_KV_RANK), (_IN_CC, 2 * CONV_CH), (_IN_KR, LANES))
CONV_HALO = 8
MLA_QH = 2 * LANES
MLA_KVH = MLA_NOPE + MLA_V


def _cparams(sem, vmem=VMEM_LIMIT_BYTES):
    return pltpu.CompilerParams(dimension_semantics=sem, vmem_limit_bytes=vmem)


def _resident(shape, index_map):
    return pl.BlockSpec(shape, index_map, pipeline_mode=pl.Buffered(1))


def _rms(x, g):
    ms = jnp.mean(x * x, axis=-1, keepdims=True)
    return x * lax.rsqrt(ms + EPS) * g


def _dot(a, b):
    return jnp.dot(a, b, preferred_element_type=F32)


def _dot_nt(a, b):
    return lax.dot_general(a, b, (((1,), (1,)), ((), ())), preferred_element_type=F32)


def _rows_to_slab(x):
    n, d = x.shape
    s = d // LANES
    y = jnp.stack([x[:, LANES * j:LANES * (j + 1)].reshape(n // SUBLANES, SUBLANES, LANES)
                   for j in range(s)], axis=1)
    return jnp.swapaxes(y, 1, 2).reshape(n, s, LANES)


def _slab_to_rows(x3):
    n, s, _ = x3.shape
    y = jnp.swapaxes(x3.reshape(n // SUBLANES, SUBLANES, s, LANES), 1, 2)
    return jnp.concatenate([y[:, j].reshape(n, LANES) for j in range(s)], axis=-1)


def _silu(x):
    return x * jax.nn.sigmoid(x)


def _rope(x, cos, sin_signed, hb):
    lane = lax.broadcasted_iota(jnp.int32, x.shape, 1)
    partner = jnp.where((lane % (2 * hb)) < hb,
                        pltpu.roll(x, LANES - hb, axis=1), pltpu.roll(x, hb, axis=1))
    return x * cos + partner * sin_signed


def _mod_kernel(c_ref, w_ref, b_ref, o_ref):
    c = c_ref[...]
    a = (c * jax.nn.sigmoid(c)).astype(BF16)
    o_ref[...] = _dot(a, w_ref[...].astype(BF16)) + b_ref[...]


def _modulation(cond, w_mod, b_mod, tn=1024):
    depth, d, n = w_mod.shape
    rows = cond.shape[0]
    return pl.pallas_call(
        _mod_kernel,
        grid=(depth, n // tn),
        in_specs=[
            pl.BlockSpec((rows, d), lambda l, j: (0, 0)),
            pl.BlockSpec((None, d, tn), lambda l, j: (l, 0, j)),
            pl.BlockSpec((None, 1, tn), lambda l, j: (l, 0, j)),
        ],
        out_specs=pl.BlockSpec((None, rows, tn), lambda l, j: (l, 0, j)),
        out_shape=jax.ShapeDtypeStruct((depth, rows, n), F32),
        compiler_params=_cparams(("parallel", "parallel")),
        name="modulation",
    )(cond, w_mod, b_mod.reshape(depth, 1, n))


def _in_proj_kernel(x_ref, g_ref, sh_ref, sc_ref, w_ref, zq_ref, zs_ref):
    h = _rms(x_ref[...], g_ref[...]) * (1.0 + sc_ref[...]) + sh_ref[...]
    z = _dot(h.astype(BF16), w_ref[...])
    for ref, pieces in ((zq_ref, ZQ_PIECES), (zs_ref, ZS_PIECES)):
        dst = 0
        for src, width in pieces:
            ref[:, dst:dst + width] = z[:, src:src + width]
            dst += width


def _in_proj(x, mod5, g_mix, w_in, layer, row_fn, tm):
    t, d = x.shape
    n = w_in.shape[-1]
    mspec = lambda chunk: pl.BlockSpec((None, None, None, 1, d),
                                       lambda i: (layer, row_fn(i), chunk, 0, 0))
    return pl.pallas_call(
        _in_proj_kernel,
        grid=(t // tm,),
        in_specs=[
            pl.BlockSpec((tm, d), lambda i: (i, 0)),
            pl.BlockSpec((None, 1, d), lambda i: (layer, 0, 0)),
            mspec(0), mspec(1),
            _resident((None, d, n), lambda i: (layer, 0, 0)),
        ],
        out_specs=[pl.BlockSpec((tm, ZQ_W), lambda i: (i, 0)),
                   pl.BlockSpec((tm, ZS_W), lambda i: (i, 0))],
        out_shape=[jax.ShapeDtypeStruct((t, ZQ_W), F32), jax.ShapeDtypeStruct((t, ZS_W), F32)],
        compiler_params=_cparams(("parallel",)),
        name="in_proj",
    )(x, g_mix.reshape(-1, 1, d), mod5, mod5, w_in)


def _mixer_kernel(*refs, seq, past, qb, latent):
    it = iter(refs)
    zq_ref, zs_ref, x_ref = next(it), next(it), next(it)
    gt1_ref, sh2_ref, sc2_ref = next(it), next(it), next(it)
    gq_ref, gk_ref, gmq_ref, gmkv_ref = next(it), next(it), next(it), next(it)
    convw_ref, ggrp_ref, gffn_ref = next(it), next(it), next(it)
    wuq_ref, wukv_ref, wout_ref, wrt_ref = next(it), next(it), next(it), next(it)
    if latent:
        ck_ref, cv_ref, cckv_ref, ckpe_ref = next(it), next(it), next(it), next(it)
        cosa_ref, sina_ref, cosc_ref, sinc_ref = next(it), next(it), next(it), next(it)
    xo_ref, h2_ref, lg_ref = next(it), next(it), next(it)
    if not latent:
        nk_ref, nv_ref, nckv_ref, nkpe_ref = next(it), next(it), next(it), next(it)
    kbf_ref, vbf_ref, kvm_ref, kpe_ref, conv_ref, u_ref = (next(it) for _ in range(6))

    j = pl.program_id(1)

    @pl.when(j == 0)
    def _per_sequence():
        if latent:
            kbf_ref[:past, :] = ck_ref[...].astype(BF16)
            vbf_ref[:past, :] = cv_ref[...].astype(BF16)
            kpe_ref[:past, :MLA_ROPE] = ckpe_ref[...].astype(BF16)
            kpe_ref[:past, MLA_ROPE:] = jnp.zeros((past, LANES - MLA_ROPE), BF16)
            for c0 in range(0, past, qb):
                c1 = min(c0 + qb, past)
                kvm_ref[c0:c1, :] = _dot(cckv_ref[c0:c1, :].astype(BF16), wukv_ref[...]).astype(BF16)
        u_ref[:CONV_HALO, :] = jnp.zeros((CONV_HALO, CONV_CH), F32)
        u_ref[CONV_HALO + seq:, :] = jnp.zeros((CONV_HALO, CONV_CH), F32)
        for c0 in range(0, seq, qb):
            rows, prow = slice(c0, c0 + qb), slice(past + c0, past + c0 + qb)
            for hk in range(GQA_KV_HEADS):
                sl = slice(hk * HEAD_DIM, (hk + 1) * HEAD_DIM)
                k = _rms(zs_ref[rows, ZS_K + hk * HEAD_DIM:ZS_K + (hk + 1) * HEAD_DIM], gk_ref[...])
                if latent:
                    k = _rope(k, cosa_ref[rows, :], sina_ref[rows, :], HEAD_DIM // 4)
                else:
                    nk_ref[rows, sl] = k
                kbf_ref[prow, sl] = k.astype(BF16)
            v = zs_ref[rows, ZS_V:ZS_V + KV_W]
            vbf_ref[prow, :] = v.astype(BF16)
            ckv_n = _rms(zs_ref[rows, ZS_CKV:ZS_CKV + MLA_KV_RANK], gmkv_ref[...])
            kpe = zs_ref[rows, ZS_KR:ZS_KR + LANES]
            if latent:
                kpe = _rope(kpe, cosc_ref[rows, :], sinc_ref[rows, :], MLA_ROPE // 4)
            else:
                nv_ref[rows, :] = v
                nckv_ref[rows, :] = ckv_n
                nkpe_ref[rows, :] = kpe[:, :MLA_ROPE]
            kpe_ref[prow, :] = kpe.astype(BF16)
            kvm_ref[prow, :] = _dot(ckv_n.astype(BF16), wukv_ref[...]).astype(BF16)
            u_ref[CONV_HALO + c0:CONV_HALO + c0 + qb, :] = (
                zs_ref[rows, ZS_CC:ZS_CC + CONV_CH] * zs_ref[rows, ZS_CH:ZS_CH + CONV_CH])
        for c0 in range(0, seq, qb):
            taps = [u_ref[CONV_HALO - 1 + c0 + i:CONV_HALO - 1 + c0 + i + qb, :] * convw_ref[i:i + 1, :]
                    for i in range(3)]
            conv_ref[c0:c0 + qb, :] = taps[0] + taps[1] + taps[2]

    r0 = pl.multiple_of(j * qb, qb)

    def attend(s, v_bf):
        m = jnp.max(s, axis=-1, keepdims=True)
        e = jnp.exp(s - m)
        return _dot(e.astype(BF16), v_bf) / jnp.sum(e, axis=-1, keepdims=True)

    group = GQA_HEADS // GQA_KV_HEADS
    outs_a = []
    for hk in range(GQA_KV_HEADS):
        qs = []
        for h in range(hk * group, (hk + 1) * group):
            q = _rms(zq_ref[:, h * HEAD_DIM:(h + 1) * HEAD_DIM], gq_ref[...])
            if latent:
                q = _rope(q, cosa_ref[pl.ds(r0, qb), :], sina_ref[pl.ds(r0, qb), :], HEAD_DIM // 4)
            qs.append(q.astype(BF16))
        ksl = slice(hk * HEAD_DIM, (hk + 1) * HEAD_DIM)
        s = _dot_nt(jnp.concatenate(qs, axis=0), kbf_ref[:, ksl]) * (HEAD_DIM ** -0.5)
        o = attend(s, vbf_ref[:, ksl])
        outs_a += [o[g * qb:(g + 1) * qb] for g in range(group)]
    out_a = jnp.concatenate(outs_a, axis=-1)

    cq_n = _rms(zq_ref[:, GQA_WIDTH:GQA_WIDTH + MLA_Q_RANK], gmq_ref[...])
    q_m = _dot(cq_n.astype(BF16), wuq_ref[...])
    outs_c = []
    for h in range(MLA_HEADS):
        q_nope = q_m[:, h * MLA_QH:h * MLA_QH + MLA_NOPE]
        q_pe = q_m[:, h * MLA_QH + MLA_NOPE:(h + 1) * MLA_QH]
        if latent:
            q_pe = _rope(q_pe, cosc_ref[pl.ds(r0, qb), :], sinc_ref[pl.ds(r0, qb), :], MLA_ROPE // 4)
        q_h = jnp.concatenate([q_nope.astype(BF16), q_pe.astype(BF16)], axis=-1)
        k_h = jnp.concatenate([kvm_ref[:, h * MLA_KVH:h * MLA_KVH + MLA_NOPE], kpe_ref[...]], axis=-1)
        s = _dot_nt(q_h, k_h) * ((MLA_NOPE + MLA_ROPE) ** -0.5)
        outs_c.append(attend(s, kvm_ref[:, h * MLA_KVH + MLA_NOPE:(h + 1) * MLA_KVH]))
    out_c = jnp.concatenate(outs_c, axis=-1)

    out_b = zq_ref[:, GQA_WIDTH + MLA_Q_RANK:] * conv_ref[pl.ds(r0, qb), :]

    g = ggrp_ref
    merged = jnp.concatenate([
        _rms(out_a, g[:, :GQA_WIDTH]).astype(BF16),
        _rms(out_b, g[:, GQA_WIDTH:GQA_WIDTH + CONV_CH]).astype(BF16),
        _rms(out_c, g[:, GQA_WIDTH + CONV_CH:]).astype(BF16)], axis=-1)
    x_new = x_ref[...] + gt1_ref[...] * _dot(merged, wout_ref[...])
    xo_ref[...] = x_new
    h2 = _rms(x_new, gffn_ref[...]) * (1.0 + sc2_ref[...]) + sh2_ref[...]
    h2_ref[...] = h2.astype(BF16)
    lg_ref[...] = lax.dot_general(wrt_ref[...], h2, (((1,), (1,)), ((), ())),
                                  precision=lax.Precision.HIGHEST, preferred_element_type=F32)


def _mixer(zq, zs, x, mod5, p, layer, *, n_seq, seq, mod_row_fn, cache=None, rope=None, qb=256):
    d = x.shape[1]
    t_out = n_seq * seq
    latent = cache is not None
    past = cache[0].shape[2] if latent else 0
    nq = seq // qb
    sk = past + seq

    def const(shape):
        return _resident(shape, lambda b, j: (0,) * len(shape))

    def lyr(shape):
        return _resident((None,) + shape, lambda b, j: (layer,) + (0,) * len(shape))

    mspec = lambda chunk: pl.BlockSpec((None, None, None, 1, d),
                                       lambda b, j: (layer, mod_row_fn(b), chunk, 0, 0))
    qrow = orow = lambda b, j: (b * nq + j, 0)
    seq_spec = _resident if nq > 1 else pl.BlockSpec
    in_specs = [
        pl.BlockSpec((qb, ZQ_W), qrow),
        seq_spec((seq, ZS_W), lambda b, j: (b, 0)),
        pl.BlockSpec((qb, d), qrow),
        mspec(2), mspec(3), mspec(4),
        lyr((1, HEAD_DIM)), lyr((1, HEAD_DIM)), lyr((1, MLA_Q_RANK)), lyr((1, MLA_KV_RANK)),
        lyr((3, CONV_CH)), lyr((1, d)), lyr((1, d)),
        lyr((MLA_Q_RANK, MLA_HEADS * MLA_QH)), lyr((MLA_KV_RANK, MLA_HEADS * MLA_KVH)),
        lyr((d, d)), lyr((N_EXPERTS, d)),
    ]
    args = [zq, zs, x, mod5, mod5, mod5,
            p["g_q"], p["g_k"], p["g_mla_q"], p["g_mla_kv"], p["conv_w"], p["g_grp"], p["g_ffn"],
            p["w_uq"], p["w_ukv"], p["w_out"], p["w_router_t"]]
    if latent:
        cspec = lambda w: pl.BlockSpec((None, None, past, w), lambda b, j: (b, layer, 0, 0))
        in_specs += [cspec(KV_W), cspec(KV_W), cspec(MLA_KV_RANK), cspec(MLA_ROPE)]
        in_specs += [const((seq, LANES))] * 4
        args += list(cache) + list(rope)
    out_specs = [pl.BlockSpec((qb, d), orow), pl.BlockSpec((qb, d), orow),
                 pl.BlockSpec((N_EXPERTS, qb), lambda b, j: (0, b * nq + j))]
    out_shape = [jax.ShapeDtypeStruct((t_out, d), F32), jax.ShapeDtypeStruct((t_out, d), BF16),
                 jax.ShapeDtypeStruct((N_EXPERTS, t_out), F32)]
    if not latent:
        sspec = lambda w: pl.BlockSpec((None, seq, w), lambda b, j: (b, 0, 0))
        out_specs += [sspec(KV_W), sspec(KV_W), sspec(MLA_KV_RANK), sspec(MLA_ROPE)]
        out_shape += [jax.ShapeDtypeStruct((n_seq, seq, w), F32)
                      for w in (KV_W, KV_W, MLA_KV_RANK, MLA_ROPE)]
    scratch = [pltpu.VMEM((sk, KV_W), BF16), pltpu.VMEM((sk, KV_W), BF16),
               pltpu.VMEM((sk, MLA_HEADS * MLA_KVH), BF16), pltpu.VMEM((sk, LANES), BF16),
               pltpu.VMEM((seq, CONV_CH), F32), pltpu.VMEM((seq + 2 * CONV_HALO, CONV_CH), F32)]
    return pl.pallas_call(
        functools.partial(_mixer_kernel, seq=seq, past=past, qb=qb, latent=latent),
        grid=(n_seq, nq),
        in_specs=in_specs, out_specs=out_specs, out_shape=out_shape, scratch_shapes=scratch,
        compiler_params=_cparams(("parallel", "arbitrary")),
        name="mixer_latent" if latent else "mixer_context",
    )(*args)


def _route(logits, bias):
    tc = logits.shape[1]
    shape3 = (N_GROUPS, GROUP_SIZE, tc)
    scores = jax.nn.sigmoid(logits)
    choice = (scores + bias).reshape(shape3)
    scores = scores.reshape(shape3)
    neg = -jnp.inf
    member = lax.broadcasted_iota(jnp.int32, shape3, 1)
    m1 = jnp.max(choice, axis=1, keepdims=True)
    first = jnp.min(jnp.where(choice == m1, member, GROUP_SIZE), axis=1, keepdims=True)
    m2 = jnp.max(jnp.where(member == first, neg, choice), axis=1, keepdims=True)
    gscore = m1 + m2
    gid = lax.broadcasted_iota(jnp.int32, gscore.shape, 0)
    gmask = jnp.zeros(gscore.shape, jnp.int32)
    for _ in range(TOPK_GROUPS):
        m = jnp.max(gscore, axis=0, keepdims=True)
        pick = jnp.min(jnp.where(gscore == m, gid, N_GROUPS), axis=0, keepdims=True)
        hit = gid == pick
        gmask = jnp.where(hit, 1, gmask)
        gscore = jnp.where(hit, neg, gscore)
    eid = lax.broadcasted_iota(jnp.int32, shape3, 0) * GROUP_SIZE + member
    cur = jnp.where(gmask > 0, choice, neg)
    sel = jnp.zeros(shape3, jnp.int32)
    picks = []
    for _ in range(TOP_K):
        m = jnp.max(jnp.max(cur, axis=0, keepdims=True), axis=1, keepdims=True)
        cand = jnp.where(cur == m, eid, N_EXPERTS)
        pick = jnp.min(jnp.min(cand, axis=0, keepdims=True), axis=1, keepdims=True)
        hit = eid == pick
        sel = jnp.where(hit, 1, sel)
        cur = jnp.where(hit, neg, cur)
        picks.append(pick)
    w = jnp.where(sel > 0, scores, 0.0)
    wsum = jnp.sum(jnp.sum(w, axis=0, keepdims=True), axis=1, keepdims=True)
    return eid, picks, sel, w / wsum * ROUTED_SCALE


def _sum_experts(x3):
    return jnp.sum(jnp.sum(x3, axis=0, keepdims=True), axis=1, keepdims=True)


def _two_part_specs(block, n_first, axis, rest=0):
    def index(part_index):
        def index_map(*grid_ids):
            i = grid_ids[axis]
            idx = [0] * len(block)
            idx[len(block) - 1 - rest] = part_index(i)
            return tuple(idx)
        return index_map
    return [pl.BlockSpec(block, index(lambda i: jnp.minimum(i, n_first - 1))),
            pl.BlockSpec(block, index(lambda i: jnp.maximum(i - n_first, 0)))]


def _router_kernel(lgc_ref, lgl_ref, b_ref, pos_ref, w_ref, cnt_out_ref, cnt_ref, base_ref, *, tm, n_ctx):
    phase, i = pl.program_id(0), pl.program_id(1)
    tc = lgc_ref.shape[1]
    logits = jnp.where(i < n_ctx, lgc_ref[...], lgl_ref[...])
    eid, picks, sel, comb = _route(logits, b_ref[...])
    sel = sel.astype(F32).reshape(N_EXPERTS, tc)

    @pl.when(phase == 0)
    def _count():
        @pl.when(i == 0)
        def _():
            cnt_ref[...] = jnp.zeros_like(cnt_ref)
        cnt_ref[...] += jnp.sum(sel, axis=1, keepdims=True)
        cnt_out_ref[...] = jnp.broadcast_to(cnt_ref[...], cnt_out_ref.shape)
        pos_ref[...] = jnp.zeros_like(pos_ref)
        w_ref[...] = jnp.zeros_like(w_ref)

    @pl.when(phase == 1)
    def _assign():
        @pl.when(i == 0)
        def _():
            tiles = jnp.floor((cnt_ref[...] + (tm - 1)) * (1.0 / tm))
            r = lax.broadcasted_iota(I32, (N_EXPERTS, N_EXPERTS), 0)
            c = lax.broadcasted_iota(I32, (N_EXPERTS, N_EXPERTS), 1)
            lower = (c < r).astype(BF16)
            first_tile = _dot(lower, jnp.broadcast_to(tiles, (N_EXPERTS, LANES)).astype(BF16))
            base_ref[...] = first_tile[:, :1] * tm
        r = lax.broadcasted_iota(I32, (tc, tc), 0)
        c = lax.broadcasted_iota(I32, (tc, tc), 1)
        incl = _dot(sel.astype(BF16), (r <= c).astype(BF16))
        row = (base_ref[...] + incl - 1.0).reshape(eid.shape)
        base_ref[...] += jnp.sum(sel, axis=1, keepdims=True)
        pos, wts = [], []
        for pick in picks:
            hit = eid == pick
            pos.append(_sum_experts(jnp.where(hit, row, 0.0)).reshape(1, tc))
            wts.append(_sum_experts(jnp.where(hit, comb, 0.0)).reshape(1, tc))
        pos_ref[...] = jnp.concatenate(pos, axis=0).astype(I32)
        wts = jnp.concatenate(wts + [jnp.zeros((LANES - TOP_K, tc), F32)], axis=0).T
        w_ref[...] = _rows_to_slab(jnp.concatenate(
            [jnp.broadcast_to(wts[:, k:k + 1], (tc, LANES)) for k in range(TOP_K)], axis=-1))


def _router(logits_ctx, logits_lat, b_router, layer, tm, tc=512):
    e = logits_ctx.shape[0]
    n_ctx = logits_ctx.shape[1] // tc
    t = logits_ctx.shape[1] + logits_lat.shape[1]
    return pl.pallas_call(
        functools.partial(_router_kernel, tm=tm, n_ctx=n_ctx),
        grid=(2, t // tc),
        in_specs=_two_part_specs((e, tc), n_ctx, axis=1) + [
            pl.BlockSpec((None, e, 1), lambda p, i: (layer, 0, 0))],
        out_specs=[pl.BlockSpec((TOP_K, tc), lambda p, i: (0, p * i)),
                   pl.BlockSpec((tc, TOP_K, LANES), lambda p, i: (p * i, 0, 0)),
                   pl.BlockSpec((e, LANES), lambda p, i: (0, 0))],
        out_shape=[jax.ShapeDtypeStruct((TOP_K, t), I32), jax.ShapeDtypeStruct((t, TOP_K, LANES), F32),
                   jax.ShapeDtypeStruct((e, LANES), F32)],
        scratch_shapes=[pltpu.VMEM((e, 1), F32), pltpu.VMEM((e, 1), F32)],
        compiler_params=_cparams(("arbitrary", "arbitrary")),
        name="router",
    )(logits_ctx, logits_lat, b_router.reshape(-1, e, 1))


def _tile_plan(counts, tm, n_tiles):
    tiles = (counts.astype(I32) + (tm - 1)) // tm
    ends = jnp.cumsum(tiles)
    n_active = ends[-1]
    tile = jnp.arange(n_tiles, dtype=I32)
    expert = jnp.sum((tile[:, None] >= ends[None, :]).astype(I32), axis=1)
    expert = jnp.where(tile < n_active, expert, expert[n_active - 1])
    return (ends - tiles) * tm, tiles, expert, n_active.reshape(1)


def _dispatch_kernel(row0_ref, tiles_ref, hc_ref, hl_ref, pos_ref, xs_ref, h_ref, zero_ref, pos_smem,
                     row_sem, zero_sem, pos_sem, *, tm, n_ctx):
    tt = h_ref.shape[0]
    h = jnp.where(pl.program_id(0) < n_ctx, hc_ref[...], hl_ref[...])
    h_ref[...] = _rows_to_slab(h.astype(F32)).astype(BF16)

    def zero_copy(e):
        row = row0_ref[e] + (tiles_ref[e] - 1) * tm
        return pltpu.make_async_copy(zero_ref, xs_ref.at[pl.ds(pl.multiple_of(row, tm), tm)], zero_sem)

    @pl.when(pl.program_id(0) == 0)
    def _zero_partial_tiles():
        zero_ref[...] = jnp.zeros_like(zero_ref)
        for e in range(N_EXPERTS):
            @pl.when(tiles_ref[e] > 0)
            def _():
                zero_copy(e).start()
        for e in range(N_EXPERTS):
            @pl.when(tiles_ref[e] > 0)
            def _():
                zero_copy(e).wait()

    pos_copy = pltpu.make_async_copy(pos_ref, pos_smem, pos_sem)
    pos_copy.start()
    pos_copy.wait()

    def issue(t, carry):
        for k in range(TOP_K):
            pltpu.make_async_copy(h_ref.at[t], xs_ref.at[pos_smem[k, t]],
                                  row_sem).start(priority=k % N_DMA_PRIORITIES)
        return carry

    lax.fori_loop(0, tt, issue, 0, unroll=8)
    for _ in range(TOP_K):
        pltpu.make_async_copy(h_ref, xs_ref.at[pl.ds(0, tt)], row_sem).wait()


def _dispatch(h_ctx, h_lat, pos, first_row, tiles, n_rows, tm, tt=DISPATCH_TT):
    d = h_ctx.shape[1]
    n_ctx = h_ctx.shape[0] // tt
    t = h_ctx.shape[0] + h_lat.shape[0]
    return pl.pallas_call(
        functools.partial(_dispatch_kernel, tm=tm, n_ctx=n_ctx),
        grid_spec=pltpu.PrefetchScalarGridSpec(
            num_scalar_prefetch=2,
            grid=(t // tt,),
            in_specs=_two_part_specs((tt, d), n_ctx, axis=0, rest=1) + [
                pl.BlockSpec((TOP_K, tt), lambda i, *_: (0, i))],
            out_specs=pl.BlockSpec(memory_space=pl.ANY),
            scratch_shapes=[pltpu.VMEM((tt, d // LANES, LANES), BF16), pltpu.VMEM((tm, d // LANES, LANES), BF16),
                            pltpu.SMEM((TOP_K, tt), I32),
                            pltpu.SemaphoreType.DMA, pltpu.SemaphoreType.DMA, pltpu.SemaphoreType.DMA],
        ),
        out_shape=jax.ShapeDtypeStruct((n_rows, d // LANES, LANES), BF16),
        compiler_params=_cparams(("arbitrary",)),
        name="moe_dispatch",
    )(first_row, tiles, h_ctx, h_lat, pos)


def _expert_kernel(te_ref, na_ref, xs_ref, wg_ref, wu_ref, wd_ref, o_ref, wgu_bf, wd_bf):
    i = pl.program_id(0)
    f = wg_ref.shape[1]

    @pl.when(i < na_ref[0])
    def _():
        @pl.when(jnp.logical_or(i == 0, te_ref[i] != te_ref[jnp.maximum(i - 1, 0)]))
        def _():
            wgu_bf[:, :f] = wg_ref[...].astype(BF16)
            wgu_bf[:, f:] = wu_ref[...].astype(BF16)
            wd_bf[...] = wd_ref[...].astype(BF16)

        x = _slab_to_rows(xs_ref[...].astype(F32)).astype(BF16)
        hgu = _dot(x, wgu_bf[...])
        out = _dot((_silu(hgu[:, :f]) * hgu[:, f:]).astype(BF16), wd_bf[...])
        o_ref[...] = _rows_to_slab(out).astype(BF16)


def _experts(xs, tile_expert, n_active, w_gate, w_up, w_down, layer, tm):
    n_rows, s, _ = xs.shape
    d, f = w_gate.shape[-2:]
    row = lambda i, te, na: (jnp.minimum(i, na[0] - 1), 0, 0)
    wspec = lambda shape: pl.BlockSpec((None, None) + shape, lambda i, te, na: (layer, te[i], 0, 0))
    return pl.pallas_call(
        _expert_kernel,
        grid_spec=pltpu.PrefetchScalarGridSpec(
            num_scalar_prefetch=2,
            grid=(n_rows // tm,),
            in_specs=[pl.BlockSpec((tm, s, LANES), row), wspec((d, f)), wspec((d, f)), wspec((f, d))],
            out_specs=pl.BlockSpec((tm, s, LANES), row),
            scratch_shapes=[pltpu.VMEM((d, 2 * f), BF16), pltpu.VMEM((f, d), BF16)],
        ),
        out_shape=jax.ShapeDtypeStruct((n_rows, s, LANES), BF16),
        compiler_params=_cparams(("arbitrary",)),
        name="moe_experts",
    )(tile_expert, n_active, xs, w_gate, w_up, w_down)


def _combine_kernel(pos_ref, w_ref, h_ref, x_ref, gt2_ref, sg_ref, su_ref, sd_ref, gfin_ref, ys_ref,
                    o_ref, buf_ref, pos_smem, row_sem, pos_sem, *, final):
    tt = x_ref.shape[0]
    pos_copy = pltpu.make_async_copy(pos_ref, pos_smem, pos_sem)
    pos_copy.start()
    pos_copy.wait()

    def issue(t, carry):
        for k in range(TOP_K):
            pltpu.make_async_copy(ys_ref.at[pos_smem[k, t]], buf_ref.at[k, t],
                                  row_sem).start(priority=k % N_DMA_PRIORITIES)
        return carry

    lax.fori_loop(0, tt, issue, 0, unroll=8)
    h = h_ref[...]
    shared = _dot((_silu(_dot(h, sg_ref[...])) * _dot(h, su_ref[...])).astype(BF16), sd_ref[...])
    for k in range(TOP_K):
        pltpu.make_async_copy(ys_ref.at[pl.ds(0, tt)], buf_ref.at[k], row_sem).wait()
    routed = buf_ref[0].astype(F32) * w_ref[:, 0:1, :]
    for k in range(1, TOP_K):
        routed = routed + buf_ref[k].astype(F32) * w_ref[:, k:k + 1, :]
    y = x_ref[...] + gt2_ref[...] * (_slab_to_rows(routed) + shared)
    o_ref[...] = _rms(y, gfin_ref[...]) if final else y


def _combine(pos, w, h2, x, mod5, ys, p, g_final, layer, row_fn, row0, final, tt=COMBINE_TT):
    t, d = x.shape
    f = p["ws_gate"].shape[-1]
    tile0 = row0 // tt
    tok = lambda i: (i, 0)
    lyr = lambda shape: _resident((None,) + shape, lambda i: (layer,) + (0,) * len(shape))
    return pl.pallas_call(
        functools.partial(_combine_kernel, final=final),
        grid=(t // tt,),
        in_specs=[
            pl.BlockSpec((TOP_K, tt), lambda i: (0, tile0 + i)),
            pl.BlockSpec((tt, TOP_K, LANES), lambda i: (tile0 + i, 0, 0)),
            pl.BlockSpec((tt, d), tok), pl.BlockSpec((tt, d), tok),
            pl.BlockSpec((None, None, None, 1, d), lambda i: (layer, row_fn(i), 5, 0, 0)),
            lyr((d, f)), lyr((d, f)), lyr((f, d)),
            _resident((1, d), lambda i: (0, 0)),
            pl.BlockSpec(memory_space=pl.ANY),
        ],
        out_specs=pl.BlockSpec((tt, d), tok),
        out_shape=jax.ShapeDtypeStruct((t, d), F32),
        scratch_shapes=[pltpu.VMEM((TOP_K, tt, d // LANES, LANES), BF16), pltpu.SMEM((TOP_K, tt), I32),
                        pltpu.SemaphoreType.DMA, pltpu.SemaphoreType.DMA],
        compiler_params=_cparams(("arbitrary",)),
        name="moe_combine",
    )(pos, w, h2, x, mod5, p["ws_gate"], p["ws_up"], p["ws_down"], g_final.reshape(1, d), ys)


def _rope_tables(n_tokens, dim):
    rows = n_tokens // GRID_W
    row = jnp.repeat(jnp.arange(rows, dtype=jnp.int32), GRID_W).astype(F32)
    col = jnp.tile(jnp.arange(GRID_W, dtype=jnp.int32), rows).astype(F32)
    half = dim // 2
    inv_freq = ROPE_THETA ** (-(jnp.arange(half // 2, dtype=F32) * 2.0 / half))
    ang_r = row[:, None] * inv_freq[None, :]
    ang_c = col[:, None] * inv_freq[None, :]
    cos = jnp.concatenate([jnp.cos(ang_r)] * 2 + [jnp.cos(ang_c)] * 2, axis=-1)
    sin = jnp.concatenate([-jnp.sin(ang_r), jnp.sin(ang_r), -jnp.sin(ang_c), jnp.sin(ang_c)], axis=-1)
    pad = ((0, 0), (0, LANES - dim))
    return jnp.pad(cos, pad), jnp.pad(sin, pad)


def _prep_params(w_in, w_uq, w_ukv, w_out, w_router, w_gate, w_up, w_down, ws_gate, ws_up, ws_down):
    depth = w_in.shape[0]
    w_in_p = jnp.pad(w_in.astype(BF16), ((0, 0), (0, 0), (0, W_IN_PAD - w_in.shape[-1])))
    w_uq_r = jnp.pad(w_uq.reshape(depth, MLA_Q_RANK, MLA_HEADS, MLA_NOPE + MLA_ROPE),
                     ((0, 0), (0, 0), (0, 0), (0, MLA_QH - MLA_NOPE - MLA_ROPE)))
    w_uq_r = w_uq_r.reshape(depth, MLA_Q_RANK, MLA_HEADS * MLA_QH).astype(BF16)
    return {
        "w_in": w_in_p, "w_uq": w_uq_r, "w_ukv": w_ukv.astype(BF16), "w_out": w_out.astype(BF16),
        "w_router_t": jnp.swapaxes(w_router, 1, 2),
        "w_gate": w_gate, "w_up": w_up, "w_down": w_down,
        "ws_gate": ws_gate.astype(BF16), "ws_up": ws_up.astype(BF16), "ws_down": ws_down.astype(BF16),
    }


def kernel(x_prompt, x_sample, c, cache_gqa_k, cache_gqa_v, cache_mla_ckv, cache_mla_kpe, c_ctx, w_mod, b_mod, g_mix, w_in, g_q, g_k, conv_w, g_mla_q, g_mla_kv, w_uq, w_ukv, g_grp, w_out, g_ffn, w_router, b_router, w_gate, w_up, w_down, ws_gate, ws_up, ws_down, g_final):
    batch, seq, d = x_prompt.shape
    dec_batch, dec_seq, _ = x_sample.shape
    depth = w_mod.shape[0]
    past = cache_gqa_k.shape[2]
    t_ctx, t_lat = batch * seq, dec_batch * dec_seq
    n_moe_tiles = (t_ctx + t_lat) * TOP_K // MOE_TM + N_EXPERTS

    p = _prep_params(w_in, w_uq, w_ukv, w_out, w_router, w_gate, w_up, w_down, ws_gate, ws_up, ws_down)
    for name, val in (("g_q", g_q), ("g_k", g_k), ("g_mla_q", g_mla_q), ("g_mla_kv", g_mla_kv),
                      ("g_grp", g_grp), ("g_ffn", g_ffn)):
        p[name] = val.reshape(depth, 1, -1)
    p["conv_w"] = conv_w

    mod_rows = 8
    cond = jnp.zeros((mod_rows, d), F32).at[0].set(c_ctx).at[1:1 + dec_batch].set(c)
    mod5 = _modulation(cond, w_mod, b_mod).reshape(depth, mod_rows, 6, 1, d)
    ctx_row = lambda i: 0
    lat_row = lambda tile: (lambda i: 1 + i // (dec_seq // tile))

    cache = (cache_gqa_k.reshape(dec_batch, depth, past, KV_W),
             cache_gqa_v.reshape(dec_batch, depth, past, KV_W), cache_mla_ckv, cache_mla_kpe)
    rope = _rope_tables(dec_seq, HEAD_DIM) + _rope_tables(dec_seq, MLA_ROPE)

    xc, xl = x_prompt.reshape(t_ctx, d), x_sample.reshape(t_lat, d)
    new_ctx = []
    for layer in range(depth):
        final = layer == depth - 1
        zqc, zsc = _in_proj(xc, mod5, g_mix, p["w_in"], layer, ctx_row, IN_TM)
        zql, zsl = _in_proj(xl, mod5, g_mix, p["w_in"], layer, lat_row(IN_TM), IN_TM)
        xc, h2c, lgc, nk, nv, nckv, nkpe = _mixer(
            zqc, zsc, xc, mod5, p, layer, n_seq=batch, seq=seq, mod_row_fn=ctx_row)
        xl, h2l, lgl = _mixer(
            zql, zsl, xl, mod5, p, layer, n_seq=dec_batch, seq=dec_seq,
            mod_row_fn=lambda b: 1 + b, cache=cache, rope=rope, qb=128)
        new_ctx.append((nk, nv, nckv, nkpe))
        pos, w_tok, counts = _router(lgc, lgl, b_router, layer, MOE_TM)
        first_row, tiles, tile_expert, n_active = _tile_plan(counts[:, 0], MOE_TM, n_moe_tiles)
        xs = _dispatch(h2c, h2l, pos, first_row, tiles, n_moe_tiles * MOE_TM, MOE_TM)
        ys = _experts(xs, tile_expert, n_active, p["w_gate"], p["w_up"], p["w_down"], layer, MOE_TM)
        xc = _combine(pos, w_tok, h2c, xc, mod5, ys, p, g_final, layer, ctx_row, 0, final)
        xl = _combine(pos, w_tok, h2l, xl, mod5, ys, p, g_final, layer, lat_row(COMBINE_TT), t_ctx, final)
    stack = lambda i: jnp.stack([lc[i] for lc in new_ctx], axis=1)
    new_k = stack(0).reshape(batch, depth, seq, GQA_KV_HEADS, HEAD_DIM)
    new_v = stack(1).reshape(batch, depth, seq, GQA_KV_HEADS, HEAD_DIM)
    return (xc.reshape(batch, seq, d), xl.reshape(dec_batch, dec_seq, d), new_k, new_v, stack(2), stack(3))
```

```python
import functools

import jax
import jax.numpy as jnp
from jax import lax
from jax.experimental import pallas as pl
from jax.experimental.pallas import tpu as pltpu

F32 = jnp.float32
BF16 = jnp.bfloat16
I32 = jnp.int32

EPS = 1e-6
ROPE_THETA = 10000.0
GRID_W = 64
GQA_HEADS = 6
GQA_KV_HEADS = 2
HEAD_DIM = 128
CONV_CH = 512
MLA_HEADS = 6
MLA_Q_RANK = 512
MLA_KV_RANK = 256
MLA_NOPE = 128
MLA_ROPE = 64
MLA_V = 128
GQA_WIDTH = GQA_HEADS * HEAD_DIM
MLA_WIDTH = MLA_HEADS * MLA_V
N_EXPERTS = 64
TOP_K = 8
N_GROUPS = 8
TOPK_GROUPS = 4
GROUP_SIZE = N_EXPERTS // N_GROUPS
ROUTED_SCALE = 2.5
MOE_TM = 704
IN_TM = 512
COMBINE_TT = 256
DISPATCH_TT = 512
ZERO_ROWS = 64
LANES = 128
SUBLANES = 8
N_DMA_PRIORITIES = 2
VMEM_LIMIT_BYTES = 56 * 1024 * 1024

ZQ_W = GQA_WIDTH + MLA_Q_RANK + CONV_CH
KV_W = GQA_KV_HEADS * HEAD_DIM
ZS_W = 2 * KV_W + MLA_KV_RANK + 2 * CONV_CH + LANES
ZS_K, ZS_V, ZS_CKV = 0, KV_W, 2 * KV_W
ZS_CC = ZS_CKV + MLA_KV_RANK
ZS_CH = ZS_CC + CONV_CH
ZS_KR = ZS_CH + CONV_CH
_IN_Q, _IN_K, _IN_V, _IN_CB, _IN_CC, _IN_CH, _IN_CQ, _IN_CKV, _IN_KR = (
    0, 768, 1024, 1280, 1792, 2304, 2816, 3328, 3584)
W_IN_PAD = _IN_KR + LANES
ZQ_PIECES = ((_IN_Q, GQA_WIDTH), (_IN_CQ, MLA_Q_RANK), (_IN_CB, CONV_CH))
ZS_PIECES = ((_IN_K, 2 * KV_W), (_IN_CKV, MLA_KV_RANK), (_IN_CC, 2 * CONV_CH), (_IN_KR, LANES))
CONV_HALO = 8
MLA_QH = 2 * LANES
MLA_KVH = MLA_NOPE + MLA_V


def _cparams(sem, vmem=VMEM_LIMIT_BYTES):
    return pltpu.CompilerParams(dimension_semantics=sem, vmem_limit_bytes=vmem)


def _resident(shape, index_map):
    return pl.BlockSpec(shape, index_map, pipeline_mode=pl.Buffered(1))


def _rms(x, g):
    ms = jnp.mean(x * x, axis=-1, keepdims=True)
    return x * lax.rsqrt(ms + EPS) * g


def _dot(a, b):
    return jnp.dot(a, b, preferred_element_type=F32)


def _dot_nt(a, b):
    return lax.dot_general(a, b, (((1,), (1,)), ((), ())), preferred_element_type=F32)


def _rows_to_slab(x):
    n, d = x.shape
    s = d // LANES
    y = jnp.stack([x[:, LANES * j:LANES * (j + 1)].reshape(n // SUBLANES, SUBLANES, LANES)
                   for j in range(s)], axis=1)
    return jnp.swapaxes(y, 1, 2).reshape(n, s, LANES)


def _slab_to_rows(x3):
    n, s, _ = x3.shape
    y = jnp.swapaxes(x3.reshape(n // SUBLANES, SUBLANES, s, LANES), 1, 2)
    return jnp.concatenate([y[:, j].reshape(n, LANES) for j in range(s)], axis=-1)


def _silu(x):
    return x * jax.nn.sigmoid(x)


def _rope(x, cos, sin_signed, hb):
    lane = lax.broadcasted_iota(jnp.int32, x.shape, 1)
    partner = jnp.where((lane % (2 * hb)) < hb,
                        pltpu.roll(x, LANES - hb, axis=1), pltpu.roll(x, hb, axis=1))
    return x * cos + partner * sin_signed


def _mod_kernel(c_ref, w_ref, b_ref, o_ref):
    c = c_ref[...]
    a = (c * jax.nn.sigmoid(c)).astype(BF16)
    o_ref[...] = _dot(a, w_ref[...].astype(BF16)) + b_ref[...]


def _modulation(cond, w_mod, b_mod, tn=1024):
    depth, d, n = w_mod.shape
    rows = cond.shape[0]
    return pl.pallas_call(
        _mod_kernel,
        grid=(depth, n // tn),
        in_specs=[
            pl.BlockSpec((rows, d), lambda l, j: (0, 0)),
            pl.BlockSpec((None, d, tn), lambda l, j: (l, 0, j)),
            pl.BlockSpec((None, 1, tn), lambda l, j: (l, 0, j)),
        ],
        out_specs=pl.BlockSpec((None, rows, tn), lambda l, j: (l, 0, j)),
        out_shape=jax.ShapeDtypeStruct((depth, rows, n), F32),
        compiler_params=_cparams(("parallel", "parallel")),
        name="modulation",
    )(cond, w_mod, b_mod.reshape(depth, 1, n))


def _in_proj_kernel(x_ref, g_ref, sh_ref, sc_ref, w_ref, zq_ref, zs_ref):
    h = _rms(x_ref[...], g_ref[...]) * (1.0 + sc_ref[...]) + sh_ref[...]
    z = _dot(h.astype(BF16), w_ref[...])
    for ref, pieces in ((zq_ref, ZQ_PIECES), (zs_ref, ZS_PIECES)):
        dst = 0
        for src, width in pieces:
            ref[:, dst:dst + width] = z[:, src:src + width]
            dst += width


def _in_proj(x, mod5, g_mix, w_in, layer, row_fn, tm):
    t, d = x.shape
    n = w_in.shape[-1]
    mspec = lambda chunk: pl.BlockSpec((None, None, None, 1, d),
                                       lambda i: (layer, row_fn(i), chunk, 0, 0))
    return pl.pallas_call(
        _in_proj_kernel,
        grid=(t // tm,),
        in_specs=[
            pl.BlockSpec((tm, d), lambda i: (i, 0)),
            pl.BlockSpec((None, 1, d), lambda i: (layer, 0, 0)),
            mspec(0), mspec(1),
            _resident((None, d, n), lambda i: (layer, 0, 0)),
        ],
        out_specs=[pl.BlockSpec((tm, ZQ_W), lambda i: (i, 0)),
                   pl.BlockSpec((tm, ZS_W), lambda i: (i, 0))],
        out_shape=[jax.ShapeDtypeStruct((t, ZQ_W), F32), jax.ShapeDtypeStruct((t, ZS_W), F32)],
        compiler_params=_cparams(("parallel",)),
        name="in_proj",
    )(x, g_mix.reshape(-1, 1, d), mod5, mod5, w_in)


def _mixer_kernel(*refs, seq, past, qb, latent):
    it = iter(refs)
    zq_ref, zs_ref, x_ref = next(it), next(it), next(it)
    gt1_ref, sh2_ref, sc2_ref = next(it), next(it), next(it)
    gq_ref, gk_ref, gmq_ref, gmkv_ref = next(it), next(it), next(it), next(it)
    convw_ref, ggrp_ref, gffn_ref = next(it), next(it), next(it)
    wuq_ref, wukv_ref, wout_ref, wrt_ref = next(it), next(it), next(it), next(it)
    if latent:
        ck_ref, cv_ref, cckv_ref, ckpe_ref = next(it), next(it), next(it), next(it)
        cosa_ref, sina_ref, cosc_ref, sinc_ref = next(it), next(it), next(it), next(it)
    xo_ref, h2_ref, lg_ref = next(it), next(it), next(it)
    if not latent:
        nk_ref, nv_ref, nckv_ref, nkpe_ref = next(it), next(it), next(it), next(it)
    kbf_ref, vbf_ref, kvm_ref, kpe_ref, conv_ref, u_ref = (next(it) for _ in range(6))

    j = pl.program_id(1)

    single = seq == qb and past == 0
    held = {}

    def keep(name, ref, index, value):
        if single:
            held[name] = held.get(name, ()) + (value,)
        else:
            ref[index] = value

    def _per_sequence():
        if latent:
            kbf_ref[:past, :] = ck_ref[...].astype(BF16)
            vbf_ref[:past, :] = cv_ref[...].astype(BF16)
            kpe_ref[:past, :MLA_ROPE] = ckpe_ref[...].astype(BF16)
            kpe_ref[:past, MLA_ROPE:] = jnp.zeros((past, LANES - MLA_ROPE), BF16)
            for c0 in range(0, past, qb):
                c1 = min(c0 + qb, past)
                kvm_ref[c0:c1, :] = _dot(cckv_ref[c0:c1, :].astype(BF16), wukv_ref[...]).astype(BF16)
        u_ref[:CONV_HALO, :] = jnp.zeros((CONV_HALO, CONV_CH), F32)
        u_ref[CONV_HALO + seq:, :] = jnp.zeros((CONV_HALO, CONV_CH), F32)
        for c0 in range(0, seq, qb):
            rows, prow = slice(c0, c0 + qb), slice(past + c0, past + c0 + qb)
            for hk in range(GQA_KV_HEADS):
                sl = slice(hk * HEAD_DIM, (hk + 1) * HEAD_DIM)
                k = _rms(zs_ref[rows, ZS_K + hk * HEAD_DIM:ZS_K + (hk + 1) * HEAD_DIM], gk_ref[...])
                if latent:
                    k = _rope(k, cosa_ref[rows, :], sina_ref[rows, :], HEAD_DIM // 4)
                else:
                    nk_ref[rows, sl] = k
                keep("k", kbf_ref, (prow, sl), k.astype(BF16))
            v = zs_ref[rows, ZS_V:ZS_V + KV_W]
            keep("v", vbf_ref, (prow, slice(None)), v.astype(BF16))
            ckv_n = _rms(zs_ref[rows, ZS_CKV:ZS_CKV + MLA_KV_RANK], gmkv_ref[...])
            kpe = zs_ref[rows, ZS_KR:ZS_KR + LANES]
            if latent:
                kpe = _rope(kpe, cosc_ref[rows, :], sinc_ref[rows, :], MLA_ROPE // 4)
            else:
                nv_ref[rows, :] = v
                nckv_ref[rows, :] = ckv_n
                nkpe_ref[rows, :] = kpe[:, :MLA_ROPE]
            keep("kpe", kpe_ref, (prow, slice(None)), kpe.astype(BF16))
            keep("kvm", kvm_ref, (prow, slice(None)), _dot(ckv_n.astype(BF16), wukv_ref[...]).astype(BF16))
            u_ref[CONV_HALO + c0:CONV_HALO + c0 + qb, :] = (
                zs_ref[rows, ZS_CC:ZS_CC + CONV_CH] * zs_ref[rows, ZS_CH:ZS_CH + CONV_CH])
        for c0 in range(0, seq, qb):
            taps = [u_ref[CONV_HALO - 1 + c0 + i:CONV_HALO - 1 + c0 + i + qb, :] * convw_ref[i:i + 1, :]
                    for i in range(3)]
            keep("conv", conv_ref, (slice(c0, c0 + qb), slice(None)), taps[0] + taps[1] + taps[2])

    r0 = pl.multiple_of(j * qb, qb)
    if single:
        _per_sequence()
        k_all, v_all = jnp.concatenate(held["k"], axis=-1), held["v"][0]
        kvm_all, kpe_all, conv_q = held["kvm"][0], held["kpe"][0], held["conv"][0]
    else:
        pl.when(j == 0)(_per_sequence)
        k_all, v_all, kvm_all, kpe_all = kbf_ref, vbf_ref, kvm_ref, kpe_ref
        conv_q = conv_ref[pl.ds(r0, qb), :]

    def attend(s, v_bf):
        m = jnp.max(s, axis=-1, keepdims=True)
        e = jnp.exp(s - m)
        return _dot(e.astype(BF16), v_bf) / jnp.sum(e, axis=-1, keepdims=True)

    group = GQA_HEADS // GQA_KV_HEADS
    outs_a = []
    for hk in range(GQA_KV_HEADS):
        qs = []
        for h in range(hk * group, (hk + 1) * group):
            q = _rms(zq_ref[:, h * HEAD_DIM:(h + 1) * HEAD_DIM], gq_ref[...])
            if latent:
                q = _rope(q, cosa_ref[pl.ds(r0, qb), :], sina_ref[pl.ds(r0, qb), :], HEAD_DIM // 4)
            qs.append(q.astype(BF16))
        ksl = slice(hk * HEAD_DIM, (hk + 1) * HEAD_DIM)
        s = _dot_nt(jnp.concatenate(qs, axis=0), k_all[:, ksl]) * (HEAD_DIM ** -0.5)
        o = attend(s, v_all[:, ksl])
        outs_a += [o[g * qb:(g + 1) * qb] for g in range(group)]
    out_a = jnp.concatenate(outs_a, axis=-1)

    cq_n = _rms(zq_ref[:, GQA_WIDTH:GQA_WIDTH + MLA_Q_RANK], gmq_ref[...])
    q_m = _dot(cq_n.astype(BF16), wuq_ref[...])
    outs_c = []
    for h in range(MLA_HEADS):
        q_nope = q_m[:, h * MLA_QH:h * MLA_QH + MLA_NOPE]
        q_pe = q_m[:, h * MLA_QH + MLA_NOPE:(h + 1) * MLA_QH]
        if latent:
            q_pe = _rope(q_pe, cosc_ref[pl.ds(r0, qb), :], sinc_ref[pl.ds(r0, qb), :], MLA_ROPE // 4)
        q_h = jnp.concatenate([q_nope.astype(BF16), q_pe.astype(BF16)], axis=-1)
        k_h = jnp.concatenate([kvm_all[:, h * MLA_KVH:h * MLA_KVH + MLA_NOPE], kpe_all[...]], axis=-1)
        s = _dot_nt(q_h, k_h) * ((MLA_NOPE + MLA_ROPE) ** -0.5)
        outs_c.append(attend(s, kvm_all[:, h * MLA_KVH + MLA_NOPE:(h + 1) * MLA_KVH]))
    out_c = jnp.concatenate(outs_c, axis=-1)

    out_b = zq_ref[:, GQA_WIDTH + MLA_Q_RANK:] * conv_q

    g = ggrp_ref
    merged = jnp.concatenate([
        _rms(out_a, g[:, :GQA_WIDTH]).astype(BF16),
        _rms(out_b, g[:, GQA_WIDTH:GQA_WIDTH + CONV_CH]).astype(BF16),
        _rms(out_c, g[:, GQA_WIDTH + CONV_CH:]).astype(BF16)], axis=-1)
    x_new = x_ref[...] + gt1_ref[...] * _dot(merged, wout_ref[...])
    xo_ref[...] = x_new
    h2 = _rms(x_new, gffn_ref[...]) * (1.0 + sc2_ref[...]) + sh2_ref[...]
    h2_ref[...] = h2.astype(BF16)
    lg_ref[...] = lax.dot_general(wrt_ref[...], h2, (((1,), (1,)), ((), ())),
                                  precision=lax.Precision.HIGHEST, preferred_element_type=F32)


def _mixer(zq, zs, x, mod5, p, layer, *, n_seq, seq, mod_row_fn, cache=None, rope=None, qb=256):
    d = x.shape[1]
    t_out = n_seq * seq
    latent = cache is not None
    past = cache[0].shape[2] if latent else 0
    nq = seq // qb
    sk = past + seq

    def const(shape):
        return _resident(shape, lambda b, j: (0,) * len(shape))

    def lyr(shape):
        return _resident((None,) + shape, lambda b, j: (layer,) + (0,) * len(shape))

    mspec = lambda chunk: pl.BlockSpec((None, None, None, 1, d),
                                       lambda b, j: (layer, mod_row_fn(b), chunk, 0, 0))
    qrow = orow = lambda b, j: (b * nq + j, 0)
    seq_spec = _resident if nq > 1 else pl.BlockSpec
    in_specs = [
        pl.BlockSpec((qb, ZQ_W), qrow),
        seq_spec((seq, ZS_W), lambda b, j: (b, 0)),
        pl.BlockSpec((qb, d), qrow),
        mspec(2), mspec(3), mspec(4),
        lyr((1, HEAD_DIM)), lyr((1, HEAD_DIM)), lyr((1, MLA_Q_RANK)), lyr((1, MLA_KV_RANK)),
        lyr((3, CONV_CH)), lyr((1, d)), lyr((1, d)),
        lyr((MLA_Q_RANK, MLA_HEADS * MLA_QH)), lyr((MLA_KV_RANK, MLA_HEADS * MLA_KVH)),
        lyr((d, d)), lyr((N_EXPERTS, d)),
    ]
    args = [zq, zs, x, mod5, mod5, mod5,
            p["g_q"], p["g_k"], p["g_mla_q"], p["g_mla_kv"], p["conv_w"], p["g_grp"], p["g_ffn"],
            p["w_uq"], p["w_ukv"], p["w_out"], p["w_router_t"]]
    if latent:
        cspec = lambda w: pl.BlockSpec((None, None, past, w), lambda b, j: (b, layer, 0, 0))
        in_specs += [cspec(KV_W), cspec(KV_W), cspec(MLA_KV_RANK), cspec(MLA_ROPE)]
        in_specs += [const((seq, LANES))] * 4
        args += list(cache) + list(rope)
    out_specs = [pl.BlockSpec((qb, d), orow), pl.BlockSpec((qb, d), orow),
                 pl.BlockSpec((N_EXPERTS, qb), lambda b, j: (0, b * nq + j))]
    out_shape = [jax.ShapeDtypeStruct((t_out, d), F32), jax.ShapeDtypeStruct((t_out, d), BF16),
                 jax.ShapeDtypeStruct((N_EXPERTS, t_out), F32)]
    if not latent:
        sspec = lambda w: pl.BlockSpec((None, seq, w), lambda b, j: (b, 0, 0))
        out_specs += [sspec(KV_W), sspec(KV_W), sspec(MLA_KV_RANK), sspec(MLA_ROPE)]
        out_shape += [jax.ShapeDtypeStruct((n_seq, seq, w), F32)
                      for w in (KV_W, KV_W, MLA_KV_RANK, MLA_ROPE)]
    scratch = [pltpu.VMEM((sk, KV_W), BF16), pltpu.VMEM((sk, KV_W), BF16),
               pltpu.VMEM((sk, MLA_HEADS * MLA_KVH), BF16), pltpu.VMEM((sk, LANES), BF16),
               pltpu.VMEM((seq, CONV_CH), F32), pltpu.VMEM((seq + 2 * CONV_HALO, CONV_CH), F32)]
    return pl.pallas_call(
        functools.partial(_mixer_kernel, seq=seq, past=past, qb=qb, latent=latent),
        grid=(n_seq, nq),
        in_specs=in_specs, out_specs=out_specs, out_shape=out_shape, scratch_shapes=scratch,
        compiler_params=_cparams(("parallel", "arbitrary")),
        name="mixer_latent" if latent else "mixer_context",
    )(*args)


def _route(logits, bias):
    tc = logits.shape[1]
    shape3 = (N_GROUPS, GROUP_SIZE, tc)
    scores = jax.nn.sigmoid(logits)
    choice = (scores + bias).reshape(shape3)
    scores = scores.reshape(shape3)
    neg = -jnp.inf
    member = lax.broadcasted_iota(jnp.int32, shape3, 1)
    m1 = jnp.max(choice, axis=1, keepdims=True)
    first = jnp.min(jnp.where(choice == m1, member, GROUP_SIZE), axis=1, keepdims=True)
    m2 = jnp.max(jnp.where(member == first, neg, choice), axis=1, keepdims=True)
    gscore = m1 + m2
    gid = lax.broadcasted_iota(jnp.int32, gscore.shape, 0)
    gmask = jnp.zeros(gscore.shape, jnp.int32)
    for _ in range(TOPK_GROUPS):
        m = jnp.max(gscore, axis=0, keepdims=True)
        pick = jnp.min(jnp.where(gscore == m, gid, N_GROUPS), axis=0, keepdims=True)
        hit = gid == pick
        gmask = jnp.where(hit, 1, gmask)
        gscore = jnp.where(hit, neg, gscore)
    eid = lax.broadcasted_iota(jnp.int32, shape3, 0) * GROUP_SIZE + member
    cur = jnp.where(gmask > 0, choice, neg)
    sel = jnp.zeros(shape3, jnp.int32)
    picks = []
    for _ in range(TOP_K):
        m = jnp.max(jnp.max(cur, axis=0, keepdims=True), axis=1, keepdims=True)
        cand = jnp.where(cur == m, eid, N_EXPERTS)
        pick = jnp.min(jnp.min(cand, axis=0, keepdims=True), axis=1, keepdims=True)
        hit = eid == pick
        sel = jnp.where(hit, 1, sel)
        cur = jnp.where(hit, neg, cur)
        picks.append(pick)
    w = jnp.where(sel > 0, scores, 0.0)
    wsum = jnp.sum(jnp.sum(w, axis=0, keepdims=True), axis=1, keepdims=True)
    return eid, picks, sel, w / wsum * ROUTED_SCALE


def _sum_experts(x3):
    return jnp.sum(jnp.sum(x3, axis=0, keepdims=True), axis=1, keepdims=True)


def _two_part_specs(block, n_first, axis, rest=0):
    def index(part_index):
        def index_map(*grid_ids):
            i = grid_ids[axis]
            idx = [0] * len(block)
            idx[len(block) - 1 - rest] = part_index(i)
            return tuple(idx)
        return index_map
    return [pl.BlockSpec(block, index(lambda i: jnp.minimum(i, n_first - 1))),
            pl.BlockSpec(block, index(lambda i: jnp.maximum(i - n_first, 0)))]


def _router_kernel(lgc_ref, lgl_ref, b_ref, pos_ref, w_ref, cnt_out_ref, cnt_ref, base_ref, *, tm, n_ctx):
    phase, i = pl.program_id(0), pl.program_id(1)
    tc = lgc_ref.shape[1]
    logits = jnp.where(i < n_ctx, lgc_ref[...], lgl_ref[...])
    eid, picks, sel, comb = _route(logits, b_ref[...])
    sel = sel.astype(F32).reshape(N_EXPERTS, tc)

    @pl.when(phase == 0)
    def _count():
        @pl.when(i == 0)
        def _():
            cnt_ref[...] = jnp.zeros_like(cnt_ref)
        cnt_ref[...] += jnp.sum(sel, axis=1, keepdims=True)
        cnt_out_ref[...] = jnp.broadcast_to(cnt_ref[...], cnt_out_ref.shape)
        pos_ref[...] = jnp.zeros_like(pos_ref)
        w_ref[...] = jnp.zeros_like(w_ref)

    @pl.when(phase == 1)
    def _assign():
        @pl.when(i == 0)
        def _():
            padded = cnt_ref[...] + (tm - 1)
            tiles = jnp.floor(padded * (1.0 / tm))
            tiles = jnp.where((tiles + 1.0) * tm <= padded, tiles + 1.0, tiles)
            tiles = jnp.where(tiles * tm > padded, tiles - 1.0, tiles)
            r = lax.broadcasted_iota(I32, (N_EXPERTS, N_EXPERTS), 0)
            c = lax.broadcasted_iota(I32, (N_EXPERTS, N_EXPERTS), 1)
            lower = (c < r).astype(BF16)
            first_tile = _dot(lower, jnp.broadcast_to(tiles, (N_EXPERTS, LANES)).astype(BF16))
            base_ref[...] = first_tile[:, :1] * tm
        r = lax.broadcasted_iota(I32, (tc, tc), 0)
        c = lax.broadcasted_iota(I32, (tc, tc), 1)
        incl = _dot(sel.astype(BF16), (r <= c).astype(BF16))
        row = (base_ref[...] + incl - 1.0).reshape(eid.shape)
        base_ref[...] += jnp.sum(sel, axis=1, keepdims=True)
        pos, wts = [], []
        for pick in picks:
            hit = eid == pick
            pos.append(_sum_experts(jnp.where(hit, row, 0.0)).reshape(1, tc))
            wts.append(_sum_experts(jnp.where(hit, comb, 0.0)).reshape(1, tc))
        pos_ref[...] = jnp.concatenate(pos, axis=0).astype(I32)
        wts = jnp.concatenate(wts + [jnp.zeros((LANES - TOP_K, tc), F32)], axis=0).T
        w_ref[...] = _rows_to_slab(jnp.concatenate(
            [jnp.broadcast_to(wts[:, k:k + 1], (tc, LANES)) for k in range(TOP_K)], axis=-1))


def _router(logits_ctx, logits_lat, b_router, layer, tm, tc=1024):
    e = logits_ctx.shape[0]
    n_ctx = logits_ctx.shape[1] // tc
    t = logits_ctx.shape[1] + logits_lat.shape[1]
    return pl.pallas_call(
        functools.partial(_router_kernel, tm=tm, n_ctx=n_ctx),
        grid=(2, t // tc),
        in_specs=_two_part_specs((e, tc), n_ctx, axis=1) + [
            pl.BlockSpec((None, e, 1), lambda p, i: (layer, 0, 0))],
        out_specs=[pl.BlockSpec((TOP_K, tc), lambda p, i: (0, p * i)),
                   pl.BlockSpec((tc, TOP_K, LANES), lambda p, i: (p * i, 0, 0)),
                   pl.BlockSpec((e, LANES), lambda p, i: (0, 0))],
        out_shape=[jax.ShapeDtypeStruct((TOP_K, t), I32), jax.ShapeDtypeStruct((t, TOP_K, LANES), F32),
                   jax.ShapeDtypeStruct((e, LANES), F32)],
        scratch_shapes=[pltpu.VMEM((e, 1), F32), pltpu.VMEM((e, 1), F32)],
        compiler_params=_cparams(("arbitrary", "arbitrary")),
        name="router",
    )(logits_ctx, logits_lat, b_router.reshape(-1, e, 1))


def _tile_plan(counts, tm, n_tiles):
    counts = counts.astype(I32)
    tiles = (counts + (tm - 1)) // tm
    ends = jnp.cumsum(tiles)
    n_active = ends[-1]
    tile = jnp.arange(n_tiles, dtype=I32)
    expert = jnp.sum((tile[:, None] >= ends[None, :]).astype(I32), axis=1)
    expert = jnp.where(tile < n_active, expert, expert[n_active - 1])
    return counts, (ends - tiles) * tm, tiles, expert, n_active.reshape(1)


def _dispatch_kernel(row0_ref, tiles_ref, cnt_ref, hc_ref, hl_ref, pos_ref, xs_ref, h_ref, zero_ref, pos_smem,
                     row_sem, zero_sem, pos_sem, *, tm, n_ctx):
    tt = h_ref.shape[0]
    h = jnp.where(pl.program_id(0) < n_ctx, hc_ref[...], hl_ref[...])
    h_ref[...] = _rows_to_slab(h.astype(F32)).astype(BF16)

    def zero_block(e, b, action):
        start = row0_ref[e] + (tiles_ref[e] - 1) * tm + b * ZERO_ROWS

        @pl.when(jnp.logical_and(tiles_ref[e] > 0, start + ZERO_ROWS > row0_ref[e] + cnt_ref[e]))
        def _():
            action(pltpu.make_async_copy(
                zero_ref, xs_ref.at[pl.ds(pl.multiple_of(start, ZERO_ROWS), ZERO_ROWS)], zero_sem))

    def for_all_zero_blocks(action):
        def per_expert(e, carry):
            for b in range(tm // ZERO_ROWS):
                zero_block(e, b, action)
            return carry
        lax.fori_loop(0, N_EXPERTS, per_expert, 0)

    @pl.when(pl.program_id(0) == 0)
    def _zero_padding_rows():
        zero_ref[...] = jnp.zeros_like(zero_ref)
        for_all_zero_blocks(lambda copy: copy.start())
        for_all_zero_blocks(lambda copy: copy.wait())

    pos_copy = pltpu.make_async_copy(pos_ref, pos_smem, pos_sem)
    pos_copy.start()
    pos_copy.wait()

    def issue(t, carry):
        for k in range(TOP_K):
            pltpu.make_async_copy(h_ref.at[t], xs_ref.at[pos_smem[k, t]],
                                  row_sem).start(priority=k % N_DMA_PRIORITIES)
        return carry

    lax.fori_loop(0, tt, issue, 0, unroll=8)
    for _ in range(TOP_K):
        pltpu.make_async_copy(h_ref, xs_ref.at[pl.ds(0, tt)], row_sem).wait()


def _dispatch(h_ctx, h_lat, pos, first_row, tiles, counts, n_rows, tm, tt=DISPATCH_TT):
    d = h_ctx.shape[1]
    n_ctx = h_ctx.shape[0] // tt
    t = h_ctx.shape[0] + h_lat.shape[0]
    assert tm % ZERO_ROWS == 0
    return pl.pallas_call(
        functools.partial(_dispatch_kernel, tm=tm, n_ctx=n_ctx),
        grid_spec=pltpu.PrefetchScalarGridSpec(
            num_scalar_prefetch=3,
            grid=(t // tt,),
            in_specs=_two_part_specs((tt, d), n_ctx, axis=0, rest=1) + [
                pl.BlockSpec((TOP_K, tt), lambda i, *_: (0, i))],
            out_specs=pl.BlockSpec(memory_space=pl.ANY),
            scratch_shapes=[pltpu.VMEM((tt, d // LANES, LANES), BF16),
                            pltpu.VMEM((ZERO_ROWS, d // LANES, LANES), BF16),
                            pltpu.SMEM((TOP_K, tt), I32),
                            pltpu.SemaphoreType.DMA, pltpu.SemaphoreType.DMA, pltpu.SemaphoreType.DMA],
        ),
        out_shape=jax.ShapeDtypeStruct((n_rows, d // LANES, LANES), BF16),
        compiler_params=_cparams(("arbitrary",)),
        name="moe_dispatch",
    )(first_row, tiles, counts, h_ctx, h_lat, pos)


def _expert_kernel(te_ref, na_ref, xs_ref, wg_ref, wu_ref, wd_ref, o_ref, wgu_bf, wd_bf):
    i = pl.program_id(0)
    f = wg_ref.shape[1]

    @pl.when(i < na_ref[0])
    def _():
        @pl.when(jnp.logical_or(i == 0, te_ref[i] != te_ref[jnp.maximum(i - 1, 0)]))
        def _():
            wgu_bf[:, :f] = wg_ref[...].astype(BF16)
            wgu_bf[:, f:] = wu_ref[...].astype(BF16)
            wd_bf[...] = wd_ref[...].astype(BF16)

        x = _slab_to_rows(xs_ref[...].astype(F32)).astype(BF16)
        hgu = _dot(x, wgu_bf[...])
        out = _dot((_silu(hgu[:, :f]) * hgu[:, f:]).astype(BF16), wd_bf[...])
        o_ref[...] = _rows_to_slab(out).astype(BF16)


def _experts(xs, tile_expert, n_active, w_gate, w_up, w_down, layer, tm):
    n_rows, s, _ = xs.shape
    d, f = w_gate.shape[-2:]
    row = lambda i, te, na: (jnp.minimum(i, na[0] - 1), 0, 0)
    wspec = lambda shape: pl.BlockSpec((None, None) + shape, lambda i, te, na: (layer, te[i], 0, 0))
    return pl.pallas_call(
        _expert_kernel,
        grid_spec=pltpu.PrefetchScalarGridSpec(
            num_scalar_prefetch=2,
            grid=(n_rows // tm,),
            in_specs=[pl.BlockSpec((tm, s, LANES), row), wspec((d, f)), wspec((d, f)), wspec((f, d))],
            out_specs=pl.BlockSpec((tm, s, LANES), row),
            scratch_shapes=[pltpu.VMEM((d, 2 * f), BF16), pltpu.VMEM((f, d), BF16)],
        ),
        out_shape=jax.ShapeDtypeStruct((n_rows, s, LANES), BF16),
        compiler_params=_cparams(("arbitrary",)),
        name="moe_experts",
    )(tile_expert, n_active, xs, w_gate, w_up, w_down)


def _combine_kernel(pos_ref, w_ref, h_ref, x_ref, gt2_ref, sg_ref, su_ref, sd_ref, gfin_ref, ys_ref,
                    o_ref, buf_ref, pos_smem, row_sem, pos_sem, *, final):
    tt = x_ref.shape[0]
    pos_copy = pltpu.make_async_copy(pos_ref, pos_smem, pos_sem)
    pos_copy.start()
    pos_copy.wait()

    def issue(t, carry):
        for k in range(TOP_K):
            pltpu.make_async_copy(ys_ref.at[pos_smem[k, t]], buf_ref.at[k, t],
                                  row_sem).start(priority=k % N_DMA_PRIORITIES)
        return carry

    lax.fori_loop(0, tt, issue, 0, unroll=8)
    h = h_ref[...]
    shared = _dot((_silu(_dot(h, sg_ref[...])) * _dot(h, su_ref[...])).astype(BF16), sd_ref[...])
    for k in range(TOP_K):
        pltpu.make_async_copy(ys_ref.at[pl.ds(0, tt)], buf_ref.at[k], row_sem).wait()
    routed = buf_ref[0].astype(F32) * w_ref[:, 0:1, :]
    for k in range(1, TOP_K):
        routed = routed + buf_ref[k].astype(F32) * w_ref[:, k:k + 1, :]
    y = x_ref[...] + gt2_ref[...] * (_slab_to_rows(routed) + shared)
    o_ref[...] = _rms(y, gfin_ref[...]) if final else y


def _combine(pos, w, h2, x, mod5, ys, p, g_final, layer, row_fn, row0, final, tt=COMBINE_TT):
    t, d = x.shape
    f = p["ws_gate"].shape[-1]
    tile0 = row0 // tt
    tok = lambda i: (i, 0)
    lyr = lambda shape: _resident((None,) + shape, lambda i: (layer,) + (0,) * len(shape))
    return pl.pallas_call(
        functools.partial(_combine_kernel, final=final),
        grid=(t // tt,),
        in_specs=[
            pl.BlockSpec((TOP_K, tt), lambda i: (0, tile0 + i)),
            pl.BlockSpec((tt, TOP_K, LANES), lambda i: (tile0 + i, 0, 0)),
            pl.BlockSpec((tt, d), tok), pl.BlockSpec((tt, d), tok),
            pl.BlockSpec((None, None, None, 1, d), lambda i: (layer, row_fn(i), 5, 0, 0)),
            lyr((d, f)), lyr((d, f)), lyr((f, d)),
            _resident((1, d), lambda i: (0, 0)),
            pl.BlockSpec(memory_space=pl.ANY),
        ],
        out_specs=pl.BlockSpec((tt, d), tok),
        out_shape=jax.ShapeDtypeStruct((t, d), F32),
        scratch_shapes=[pltpu.VMEM((TOP_K, tt, d // LANES, LANES), BF16), pltpu.SMEM((TOP_K, tt), I32),
                        pltpu.SemaphoreType.DMA, pltpu.SemaphoreType.DMA],
        compiler_params=_cparams(("arbitrary",)),
        name="moe_combine",
    )(pos, w, h2, x, mod5, p["ws_gate"], p["ws_up"], p["ws_down"], g_final.reshape(1, d), ys)


def _rope_tables(n_tokens, dim):
    rows = n_tokens // GRID_W
    row = jnp.repeat(jnp.arange(rows, dtype=jnp.int32), GRID_W).astype(F32)
    col = jnp.tile(jnp.arange(GRID_W, dtype=jnp.int32), rows).astype(F32)
    half = dim // 2
    inv_freq = ROPE_THETA ** (-(jnp.arange(half // 2, dtype=F32) * 2.0 / half))
    ang_r = row[:, None] * inv_freq[None, :]
    ang_c = col[:, None] * inv_freq[None, :]
    cos = jnp.concatenate([jnp.cos(ang_r)] * 2 + [jnp.cos(ang_c)] * 2, axis=-1)
    sin = jnp.concatenate([-jnp.sin(ang_r), jnp.sin(ang_r), -jnp.sin(ang_c), jnp.sin(ang_c)], axis=-1)
    pad = ((0, 0), (0, LANES - dim))
    return jnp.pad(cos, pad), jnp.pad(sin, pad)


def _prep_params(w_in, w_uq, w_ukv, w_out, w_router, w_gate, w_up, w_down, ws_gate, ws_up, ws_down):
    depth = w_in.shape[0]
    w_in_p = jnp.pad(w_in.astype(BF16), ((0, 0), (0, 0), (0, W_IN_PAD - w_in.shape[-1])))
    w_uq_r = jnp.pad(w_uq.reshape(depth, MLA_Q_RANK, MLA_HEADS, MLA_NOPE + MLA_ROPE),
                     ((0, 0), (0, 0), (0, 0), (0, MLA_QH - MLA_NOPE - MLA_ROPE)))
    w_uq_r = w_uq_r.reshape(depth, MLA_Q_RANK, MLA_HEADS * MLA_QH).astype(BF16)
    return {
        "w_in": w_in_p, "w_uq": w_uq_r, "w_ukv": w_ukv.astype(BF16), "w_out": w_out.astype(BF16),
        "w_router_t": jnp.swapaxes(w_router, 1, 2),
        "w_gate": w_gate, "w_up": w_up, "w_down": w_down,
        "ws_gate": ws_gate.astype(BF16), "ws_up": ws_up.astype(BF16), "ws_down": ws_down.astype(BF16),
    }


def kernel(x_prompt, x_sample, c, cache_gqa_k, cache_gqa_v, cache_mla_ckv, cache_mla_kpe, c_ctx, w_mod, b_mod, g_mix, w_in, g_q, g_k, conv_w, g_mla_q, g_mla_kv, w_uq, w_ukv, g_grp, w_out, g_ffn, w_router, b_router, w_gate, w_up, w_down, ws_gate, ws_up, ws_down, g_final):
    batch, seq, d = x_prompt.shape
    dec_batch, dec_seq, _ = x_sample.shape
    depth = w_mod.shape[0]
    past = cache_gqa_k.shape[2]
    t_ctx, t_lat = batch * seq, dec_batch * dec_seq
    n_moe_tiles = pl.cdiv((t_ctx + t_lat) * TOP_K, MOE_TM) + N_EXPERTS

    p = _prep_params(w_in, w_uq, w_ukv, w_out, w_router, w_gate, w_up, w_down, ws_gate, ws_up, ws_down)
    for name, val in (("g_q", g_q), ("g_k", g_k), ("g_mla_q", g_mla_q), ("g_mla_kv", g_mla_kv),
                      ("g_grp", g_grp), ("g_ffn", g_ffn)):
        p[name] = val.reshape(depth, 1, -1)
    p["conv_w"] = conv_w

    mod_rows = 8
    cond = jnp.zeros((mod_rows, d), F32).at[0].set(c_ctx).at[1:1 + dec_batch].set(c)
    mod5 = _modulation(cond, w_mod, b_mod).reshape(depth, mod_rows, 6, 1, d)
    ctx_row = lambda i: 0
    lat_row = lambda tile: (lambda i: 1 + i // (dec_seq // tile))

    cache = (cache_gqa_k.reshape(dec_batch, depth, past, KV_W),
             cache_gqa_v.reshape(dec_batch, depth, past, KV_W), cache_mla_ckv, cache_mla_kpe)
    rope = _rope_tables(dec_seq, HEAD_DIM) + _rope_tables(dec_seq, MLA_ROPE)

    xc, xl = x_prompt.reshape(t_ctx, d), x_sample.reshape(t_lat, d)
    new_ctx = []
    for layer in range(depth):
        final = layer == depth - 1
        zqc, zsc = _in_proj(xc, mod5, g_mix, p["w_in"], layer, ctx_row, IN_TM)
        zql, zsl = _in_proj(xl, mod5, g_mix, p["w_in"], layer, lat_row(IN_TM), IN_TM)
        xc, h2c, lgc, nk, nv, nckv, nkpe = _mixer(
            zqc, zsc, xc, mod5, p, layer, n_seq=batch, seq=seq, mod_row_fn=ctx_row)
        xl, h2l, lgl = _mixer(
            zql, zsl, xl, mod5, p, layer, n_seq=dec_batch, seq=dec_seq,
            mod_row_fn=lambda b: 1 + b, cache=cache, rope=rope, qb=128)
        new_ctx.append((nk, nv, nckv, nkpe))
        pos, w_tok, counts = _router(lgc, lgl, b_router, layer, MOE_TM)
        counts, first_row, tiles, tile_expert, n_active = _tile_plan(counts[:, 0], MOE_TM, n_moe_tiles)
        xs = _dispatch(h2c, h2l, pos, first_row, tiles, counts, n_moe_tiles * MOE_TM, MOE_TM)
        ys = _experts(xs, tile_expert, n_active, p["w_gate"], p["w_up"], p["w_down"], layer, MOE_TM)
        xc = _combine(pos, w_tok, h2c, xc, mod5, ys, p, g_final, layer, ctx_row, 0, final)
        xl = _combine(pos, w_tok, h2l, xl, mod5, ys, p, g_final, layer, lat_row(COMBINE_TT), t_ctx, final)
    stack = lambda i: jnp.stack([lc[i] for lc in new_ctx], axis=1)
    new_k = stack(0).reshape(batch, depth, seq, GQA_KV_HEADS, HEAD_DIM)
    new_v = stack(1).reshape(batch, depth, seq, GQA_KV_HEADS, HEAD_DIM)
    return (xc.reshape(batch, seq, d), xl.reshape(dec_batch, dec_seq, d), new_k, new_v, stack(2), stack(3))
```

```python
import functools

import jax
import jax.numpy as jnp
from jax import lax
from jax.experimental import pallas as pl
from jax.experimental.pallas import tpu as pltpu

F32 = jnp.float32
BF16 = jnp.bfloat16
I32 = jnp.int32

EPS = 1e-6
ROPE_THETA = 10000.0
GRID_W = 64
GQA_HEADS = 6
GQA_KV_HEADS = 2
HEAD_DIM = 128
CONV_CH = 512
MLA_HEADS = 6
MLA_Q_RANK = 512
MLA_KV_RANK = 256
MLA_NOPE = 128
MLA_ROPE = 64
MLA_V = 128
GQA_WIDTH = GQA_HEADS * HEAD_DIM
MLA_WIDTH = MLA_HEADS * MLA_V
N_EXPERTS = 64
TOP_K = 8
N_GROUPS = 8
TOPK_GROUPS = 4
GROUP_SIZE = N_EXPERTS // N_GROUPS
ROUTED_SCALE = 2.5
MOE_TM = 704
IN_TM = 512
COMBINE_TT = 256
DISPATCH_TT = 512
ZERO_ROWS = 64
LANES = 128
SUBLANES = 8
N_DMA_PRIORITIES = 2
VMEM_LIMIT_BYTES = 56 * 1024 * 1024

ZQ_W = GQA_WIDTH + MLA_Q_RANK + CONV_CH
KV_W = GQA_KV_HEADS * HEAD_DIM
ZS_W = 2 * KV_W + MLA_KV_RANK + 2 * CONV_CH + LANES
ZS_K, ZS_V, ZS_CKV = 0, KV_W, 2 * KV_W
ZS_CC = ZS_CKV + MLA_KV_RANK
ZS_CH = ZS_CC + CONV_CH
ZS_KR = ZS_CH + CONV_CH
_IN_Q, _IN_K, _IN_V, _IN_CB, _IN_CC, _IN_CH, _IN_CQ, _IN_CKV, _IN_KR = (
    0, 768, 1024, 1280, 1792, 2304, 2816, 3328, 3584)
W_IN_PAD = _IN_KR + LANES
ZQ_PIECES = ((_IN_Q, GQA_WIDTH), (_IN_CQ, MLA_Q_RANK), (_IN_CB, CONV_CH))
ZS_PIECES = ((_IN_K, 2 * KV_W), (_IN_CKV, MLA_KV_RANK), (_IN_CC, 2 * CONV_CH), (_IN_KR, LANES))
CONV_HALO = 8
MLA_QH = 2 * LANES
MLA_KVH = MLA_NOPE + MLA_V


def _cparams(sem, vmem=VMEM_LIMIT_BYTES):
    return pltpu.CompilerParams(dimension_semantics=sem, vmem_limit_bytes=vmem)


def _resident(shape, index_map):
    return pl.BlockSpec(shape, index_map, pipeline_mode=pl.Buffered(1))


def _rms(x, g):
    ms = jnp.mean(x * x, axis=-1, keepdims=True)
    return x * lax.rsqrt(ms + EPS) * g


def _dot(a, b):
    return jnp.dot(a, b, preferred_element_type=F32)


def _dot_nt(a, b):
    return lax.dot_general(a, b, (((1,), (1,)), ((), ())), preferred_element_type=F32)


def _rows_per_vreg(dtype):
    return SUBLANES * 4 // jnp.dtype(dtype).itemsize


def _rows_to_slab(x):
    n, d = x.shape
    s, g = d // LANES, _rows_per_vreg(x.dtype)
    y = jnp.stack([x[:, LANES * j:LANES * (j + 1)].reshape(n // g, g, LANES) for j in range(s)], axis=1)
    return jnp.swapaxes(y, 1, 2).reshape(n, s, LANES)


def _slab_to_rows(x3):
    n, s, _ = x3.shape
    g = _rows_per_vreg(x3.dtype)
    y = jnp.swapaxes(x3.reshape(n // g, g, s, LANES), 1, 2)
    return jnp.concatenate([y[:, j].reshape(n, LANES) for j in range(s)], axis=-1)


def _silu(x):
    return x * jax.nn.sigmoid(x)


def _rope(x, cos, sin_signed, hb):
    lane = lax.broadcasted_iota(jnp.int32, x.shape, 1)
    partner = jnp.where((lane % (2 * hb)) < hb,
                        pltpu.roll(x, LANES - hb, axis=1), pltpu.roll(x, hb, axis=1))
    return x * cos + partner * sin_signed


def _mod_kernel(c_ref, w_ref, b_ref, o_ref):
    c = c_ref[...]
    a = (c * jax.nn.sigmoid(c)).astype(BF16)
    o_ref[...] = _dot(a, w_ref[...].astype(BF16)) + b_ref[...]


def _modulation(cond, w_mod, b_mod, tn=1024):
    depth, d, n = w_mod.shape
    rows = cond.shape[0]
    return pl.pallas_call(
        _mod_kernel,
        grid=(depth, n // tn),
        in_specs=[
            pl.BlockSpec((rows, d), lambda l, j: (0, 0)),
            pl.BlockSpec((None, d, tn), lambda l, j: (l, 0, j)),
            pl.BlockSpec((None, 1, tn), lambda l, j: (l, 0, j)),
        ],
        out_specs=pl.BlockSpec((None, rows, tn), lambda l, j: (l, 0, j)),
        out_shape=jax.ShapeDtypeStruct((depth, rows, n), F32),
        compiler_params=_cparams(("parallel", "parallel")),
        name="modulation",
    )(cond, w_mod, b_mod.reshape(depth, 1, n))


def _in_proj_kernel(x_ref, g_ref, sh_ref, sc_ref, w_ref, zq_ref, zs_ref):
    h = _rms(x_ref[...], g_ref[...]) * (1.0 + sc_ref[...]) + sh_ref[...]
    z = _dot(h.astype(BF16), w_ref[...])
    for ref, pieces in ((zq_ref, ZQ_PIECES), (zs_ref, ZS_PIECES)):
        dst = 0
        for src, width in pieces:
            ref[:, dst:dst + width] = z[:, src:src + width]
            dst += width


def _in_proj(x, mod5, g_mix, w_in, layer, row_fn, tm):
    t, d = x.shape
    n = w_in.shape[-1]
    mspec = lambda chunk: pl.BlockSpec((None, None, None, 1, d),
                                       lambda i: (layer, row_fn(i), chunk, 0, 0))
    return pl.pallas_call(
        _in_proj_kernel,
        grid=(t // tm,),
        in_specs=[
            pl.BlockSpec((tm, d), lambda i: (i, 0)),
            pl.BlockSpec((None, 1, d), lambda i: (layer, 0, 0)),
            mspec(0), mspec(1),
            _resident((None, d, n), lambda i: (layer, 0, 0)),
        ],
        out_specs=[pl.BlockSpec((tm, ZQ_W), lambda i: (i, 0)),
                   pl.BlockSpec((tm, ZS_W), lambda i: (i, 0))],
        out_shape=[jax.ShapeDtypeStruct((t, ZQ_W), F32), jax.ShapeDtypeStruct((t, ZS_W), F32)],
        compiler_params=_cparams(("parallel",)),
        name="in_proj",
    )(x, g_mix.reshape(-1, 1, d), mod5, mod5, w_in)


def _mixer_kernel(*refs, seq, past, qb, latent):
    it = iter(refs)
    zq_ref, zs_ref, x_ref = next(it), next(it), next(it)
    gt1_ref, sh2_ref, sc2_ref = next(it), next(it), next(it)
    gq_ref, gk_ref, gmq_ref, gmkv_ref = next(it), next(it), next(it), next(it)
    convw_ref, ggrp_ref, gffn_ref = next(it), next(it), next(it)
    wuq_ref, wukv_ref, wout_ref, wrt_ref = next(it), next(it), next(it), next(it)
    if latent:
        ck_ref, cv_ref, cckv_ref, ckpe_ref = next(it), next(it), next(it), next(it)
        cosa_ref, sina_ref, cosc_ref, sinc_ref = next(it), next(it), next(it), next(it)
    xo_ref, h2_ref, lg_ref = next(it), next(it), next(it)
    if not latent:
        nk_ref, nv_ref, nckv_ref, nkpe_ref = next(it), next(it), next(it), next(it)
    kbf_ref, vbf_ref, kvm_ref, kpe_ref, conv_ref, u_ref = (next(it) for _ in range(6))

    j = pl.program_id(1)

    single = seq == qb and past == 0
    held = {}

    def keep(name, ref, index, value):
        if single:
            held[name] = held.get(name, ()) + (value,)
        else:
            ref[index] = value

    def _per_sequence():
        if latent:
            kbf_ref[:past, :] = ck_ref[...].astype(BF16)
            vbf_ref[:past, :] = cv_ref[...].astype(BF16)
            kpe_ref[:past, :MLA_ROPE] = ckpe_ref[...].astype(BF16)
            kpe_ref[:past, MLA_ROPE:] = jnp.zeros((past, LANES - MLA_ROPE), BF16)
            for c0 in range(0, past, qb):
                c1 = min(c0 + qb, past)
                kvm_ref[c0:c1, :] = _dot(cckv_ref[c0:c1, :].astype(BF16), wukv_ref[...]).astype(BF16)
        u_ref[:CONV_HALO, :] = jnp.zeros((CONV_HALO, CONV_CH), F32)
        u_ref[CONV_HALO + seq:, :] = jnp.zeros((CONV_HALO, CONV_CH), F32)
        for c0 in range(0, seq, qb):
            rows, prow = slice(c0, c0 + qb), slice(past + c0, past + c0 + qb)
            for hk in range(GQA_KV_HEADS):
                sl = slice(hk * HEAD_DIM, (hk + 1) * HEAD_DIM)
                k = _rms(zs_ref[rows, ZS_K + hk * HEAD_DIM:ZS_K + (hk + 1) * HEAD_DIM], gk_ref[...])
                if latent:
                    k = _rope(k, cosa_ref[rows, :], sina_ref[rows, :], HEAD_DIM // 4)
                else:
                    nk_ref[rows, sl] = k
                keep("k", kbf_ref, (prow, sl), k.astype(BF16))
            v = zs_ref[rows, ZS_V:ZS_V + KV_W]
            keep("v", vbf_ref, (prow, slice(None)), v.astype(BF16))
            ckv_n = _rms(zs_ref[rows, ZS_CKV:ZS_CKV + MLA_KV_RANK], gmkv_ref[...])
            kpe = zs_ref[rows, ZS_KR:ZS_KR + LANES]
            if latent:
                kpe = _rope(kpe, cosc_ref[rows, :], sinc_ref[rows, :], MLA_ROPE // 4)
            else:
                nv_ref[rows, :] = v
                nckv_ref[rows, :] = ckv_n
                nkpe_ref[rows, :] = kpe[:, :MLA_ROPE]
            keep("kpe", kpe_ref, (prow, slice(None)), kpe.astype(BF16))
            keep("kvm", kvm_ref, (prow, slice(None)), _dot(ckv_n.astype(BF16), wukv_ref[...]).astype(BF16))
            u_ref[CONV_HALO + c0:CONV_HALO + c0 + qb, :] = (
                zs_ref[rows, ZS_CC:ZS_CC + CONV_CH] * zs_ref[rows, ZS_CH:ZS_CH + CONV_CH])
        for c0 in range(0, seq, qb):
            taps = [u_ref[CONV_HALO - 1 + c0 + i:CONV_HALO - 1 + c0 + i + qb, :] * convw_ref[i:i + 1, :]
                    for i in range(3)]
            keep("conv", conv_ref, (slice(c0, c0 + qb), slice(None)), taps[0] + taps[1] + taps[2])

    r0 = pl.multiple_of(j * qb, qb)
    if single:
        _per_sequence()
        k_all, v_all = jnp.concatenate(held["k"], axis=-1), held["v"][0]
        kvm_all, kpe_all, conv_q = held["kvm"][0], held["kpe"][0], held["conv"][0]
    else:
        pl.when(j == 0)(_per_sequence)
        k_all, v_all, kvm_all, kpe_all = kbf_ref, vbf_ref, kvm_ref, kpe_ref
        conv_q = conv_ref[pl.ds(r0, qb), :]

    def attend(s, v_bf):
        m = jnp.max(s, axis=-1, keepdims=True)
        e = jnp.exp(s - m)
        return _dot(e.astype(BF16), v_bf) / jnp.sum(e, axis=-1, keepdims=True)

    group = GQA_HEADS // GQA_KV_HEADS
    outs_a = []
    for hk in range(GQA_KV_HEADS):
        qs = []
        for h in range(hk * group, (hk + 1) * group):
            q = _rms(zq_ref[:, h * HEAD_DIM:(h + 1) * HEAD_DIM], gq_ref[...])
            if latent:
                q = _rope(q, cosa_ref[pl.ds(r0, qb), :], sina_ref[pl.ds(r0, qb), :], HEAD_DIM // 4)
            qs.append(q.astype(BF16))
        ksl = slice(hk * HEAD_DIM, (hk + 1) * HEAD_DIM)
        s = _dot_nt(jnp.concatenate(qs, axis=0), k_all[:, ksl]) * (HEAD_DIM ** -0.5)
        o = attend(s, v_all[:, ksl])
        outs_a += [o[g * qb:(g + 1) * qb] for g in range(group)]
    out_a = jnp.concatenate(outs_a, axis=-1)

    cq_n = _rms(zq_ref[:, GQA_WIDTH:GQA_WIDTH + MLA_Q_RANK], gmq_ref[...])
    q_m = _dot(cq_n.astype(BF16), wuq_ref[...])
    outs_c = []
    for h in range(MLA_HEADS):
        q_nope = q_m[:, h * MLA_QH:h * MLA_QH + MLA_NOPE]
        q_pe = q_m[:, h * MLA_QH + MLA_NOPE:(h + 1) * MLA_QH]
        if latent:
            q_pe = _rope(q_pe, cosc_ref[pl.ds(r0, qb), :], sinc_ref[pl.ds(r0, qb), :], MLA_ROPE // 4)
        q_h = jnp.concatenate([q_nope.astype(BF16), q_pe.astype(BF16)], axis=-1)
        k_h = jnp.concatenate([kvm_all[:, h * MLA_KVH:h * MLA_KVH + MLA_NOPE], kpe_all[...]], axis=-1)
        s = _dot_nt(q_h, k_h) * ((MLA_NOPE + MLA_ROPE) ** -0.5)
        outs_c.append(attend(s, kvm_all[:, h * MLA_KVH + MLA_NOPE:(h + 1) * MLA_KVH]))
    out_c = jnp.concatenate(outs_c, axis=-1)

    out_b = zq_ref[:, GQA_WIDTH + MLA_Q_RANK:] * conv_q

    g = ggrp_ref
    merged = jnp.concatenate([
        _rms(out_a, g[:, :GQA_WIDTH]).astype(BF16),
        _rms(out_b, g[:, GQA_WIDTH:GQA_WIDTH + CONV_CH]).astype(BF16),
        _rms(out_c, g[:, GQA_WIDTH + CONV_CH:]).astype(BF16)], axis=-1)
    x_new = x_ref[...] + gt1_ref[...] * _dot(merged, wout_ref[...])
    xo_ref[...] = x_new
    h2 = _rms(x_new, gffn_ref[...]) * (1.0 + sc2_ref[...]) + sh2_ref[...]
    h2_ref[...] = h2.astype(BF16)
    def split(a):
        hi = a.astype(BF16)
        return hi, (a - hi.astype(F32)).astype(BF16)

    (w_hi, w_lo), (h_hi, h_lo) = split(wrt_ref[...]), split(h2)
    lg_ref[...] = _dot_nt(w_hi, h_hi) + (_dot_nt(w_hi, h_lo) + _dot_nt(w_lo, h_hi))


def _mixer(zq, zs, x, mod5, p, layer, *, n_seq, seq, mod_row_fn, cache=None, rope=None, qb=256):
    d = x.shape[1]
    t_out = n_seq * seq
    latent = cache is not None
    past = cache[0].shape[2] if latent else 0
    nq = seq // qb
    sk = past + seq

    def const(shape):
        return _resident(shape, lambda b, j: (0,) * len(shape))

    def lyr(shape):
        return _resident((None,) + shape, lambda b, j: (layer,) + (0,) * len(shape))

    mspec = lambda chunk: pl.BlockSpec((None, None, None, 1, d),
                                       lambda b, j: (layer, mod_row_fn(b), chunk, 0, 0))
    qrow = orow = lambda b, j: (b * nq + j, 0)
    seq_spec = _resident if nq > 1 else pl.BlockSpec
    in_specs = [
        pl.BlockSpec((qb, ZQ_W), qrow),
        seq_spec((seq, ZS_W), lambda b, j: (b, 0)),
        pl.BlockSpec((qb, d), qrow),
        mspec(2), mspec(3), mspec(4),
        lyr((1, HEAD_DIM)), lyr((1, HEAD_DIM)), lyr((1, MLA_Q_RANK)), lyr((1, MLA_KV_RANK)),
        lyr((3, CONV_CH)), lyr((1, d)), lyr((1, d)),
        lyr((MLA_Q_RANK, MLA_HEADS * MLA_QH)), lyr((MLA_KV_RANK, MLA_HEADS * MLA_KVH)),
        lyr((d, d)), lyr((N_EXPERTS, d)),
    ]
    args = [zq, zs, x, mod5, mod5, mod5,
            p["g_q"], p["g_k"], p["g_mla_q"], p["g_mla_kv"], p["conv_w"], p["g_grp"], p["g_ffn"],
            p["w_uq"], p["w_ukv"], p["w_out"], p["w_router_t"]]
    if latent:
        cspec = lambda w: pl.BlockSpec((None, None, past, w), lambda b, j: (b, layer, 0, 0))
        in_specs += [cspec(KV_W), cspec(KV_W), cspec(MLA_KV_RANK), cspec(MLA_ROPE)]
        in_specs += [const((seq, LANES))] * 4
        args += list(cache) + list(rope)
    out_specs = [pl.BlockSpec((qb, d), orow), pl.BlockSpec((qb, d), orow),
                 pl.BlockSpec((N_EXPERTS, qb), lambda b, j: (0, b * nq + j))]
    out_shape = [jax.ShapeDtypeStruct((t_out, d), F32), jax.ShapeDtypeStruct((t_out, d), BF16),
                 jax.ShapeDtypeStruct((N_EXPERTS, t_out), F32)]
    if not latent:
        sspec = lambda w: pl.BlockSpec((None, seq, w), lambda b, j: (b, 0, 0))
        out_specs += [sspec(KV_W), sspec(KV_W), sspec(MLA_KV_RANK), sspec(MLA_ROPE)]
        out_shape += [jax.ShapeDtypeStruct((n_seq, seq, w), F32)
                      for w in (KV_W, KV_W, MLA_KV_RANK, MLA_ROPE)]
    scratch = [pltpu.VMEM((sk, KV_W), BF16), pltpu.VMEM((sk, KV_W), BF16),
               pltpu.VMEM((sk, MLA_HEADS * MLA_KVH), BF16), pltpu.VMEM((sk, LANES), BF16),
               pltpu.VMEM((seq, CONV_CH), F32), pltpu.VMEM((seq + 2 * CONV_HALO, CONV_CH), F32)]
    return pl.pallas_call(
        functools.partial(_mixer_kernel, seq=seq, past=past, qb=qb, latent=latent),
        grid=(n_seq, nq),
        in_specs=in_specs, out_specs=out_specs, out_shape=out_shape, scratch_shapes=scratch,
        compiler_params=_cparams(("parallel", "arbitrary")),
        name="mixer_latent" if latent else "mixer_context",
    )(*args)


def _route(logits, bias):
    tc = logits.shape[1]
    shape3 = (N_GROUPS, GROUP_SIZE, tc)
    scores = jax.nn.sigmoid(logits)
    choice = (scores + bias).reshape(shape3)
    scores = scores.reshape(shape3)
    neg = -jnp.inf
    member = lax.broadcasted_iota(jnp.int32, shape3, 1)
    m1 = jnp.max(choice, axis=1, keepdims=True)
    first = jnp.min(jnp.where(choice == m1, member, GROUP_SIZE), axis=1, keepdims=True)
    m2 = jnp.max(jnp.where(member == first, neg, choice), axis=1, keepdims=True)
    gscore = m1 + m2
    gid = lax.broadcasted_iota(jnp.int32, gscore.shape, 0)
    gmask = jnp.zeros(gscore.shape, jnp.int32)
    for _ in range(TOPK_GROUPS):
        m = jnp.max(gscore, axis=0, keepdims=True)
        pick = jnp.min(jnp.where(gscore == m, gid, N_GROUPS), axis=0, keepdims=True)
        hit = gid == pick
        gmask = jnp.where(hit, 1, gmask)
        gscore = jnp.where(hit, neg, gscore)
    eid = lax.broadcasted_iota(jnp.int32, shape3, 0) * GROUP_SIZE + member
    cur = jnp.where(gmask > 0, choice, neg)
    sel = jnp.zeros(shape3, jnp.int32)
    picks = []
    for _ in range(TOP_K):
        m = jnp.max(jnp.max(cur, axis=0, keepdims=True), axis=1, keepdims=True)
        cand = jnp.where(cur == m, eid, N_EXPERTS)
        pick = jnp.min(jnp.min(cand, axis=0, keepdims=True), axis=1, keepdims=True)
        hit = eid == pick
        sel = jnp.where(hit, 1, sel)
        cur = jnp.where(hit, neg, cur)
        picks.append(pick)
    w = jnp.where(sel > 0, scores, 0.0)
    wsum = jnp.sum(jnp.sum(w, axis=0, keepdims=True), axis=1, keepdims=True)
    return eid, picks, sel, w / wsum * ROUTED_SCALE


def _sum_experts(x3):
    return jnp.sum(jnp.sum(x3, axis=0, keepdims=True), axis=1, keepdims=True)


def _two_part_specs(block, n_first, axis, rest=0):
    def index(part_index):
        def index_map(*grid_ids):
            i = grid_ids[axis]
            idx = [0] * len(block)
            idx[len(block) - 1 - rest] = part_index(i)
            return tuple(idx)
        return index_map
    return [pl.BlockSpec(block, index(lambda i: jnp.minimum(i, n_first - 1))),
            pl.BlockSpec(block, index(lambda i: jnp.maximum(i - n_first, 0)))]


def _router_kernel(lgc_ref, lgl_ref, b_ref, pos_ref, w_ref, cnt_out_ref, cnt_ref, base_ref, *, tm, n_ctx):
    phase, i = pl.program_id(0), pl.program_id(1)
    tc = lgc_ref.shape[1]
    logits = jnp.where(i < n_ctx, lgc_ref[...], lgl_ref[...])
    eid, picks, sel, comb = _route(logits, b_ref[...])
    sel = sel.astype(F32).reshape(N_EXPERTS, tc)

    @pl.when(phase == 0)
    def _count():
        @pl.when(i == 0)
        def _():
            cnt_ref[...] = jnp.zeros_like(cnt_ref)
        cnt_ref[...] += jnp.sum(sel, axis=1, keepdims=True)
        cnt_out_ref[...] = jnp.broadcast_to(cnt_ref[...], cnt_out_ref.shape)
        pos_ref[...] = jnp.zeros_like(pos_ref)
        w_ref[...] = jnp.zeros_like(w_ref)

    @pl.when(phase == 1)
    def _assign():
        @pl.when(i == 0)
        def _():
            padded = cnt_ref[...] + (tm - 1)
            tiles = jnp.floor(padded * (1.0 / tm))
            tiles = jnp.where((tiles + 1.0) * tm <= padded, tiles + 1.0, tiles)
            tiles = jnp.where(tiles * tm > padded, tiles - 1.0, tiles)
            r = lax.broadcasted_iota(I32, (N_EXPERTS, N_EXPERTS), 0)
            c = lax.broadcasted_iota(I32, (N_EXPERTS, N_EXPERTS), 1)
            lower = (c < r).astype(BF16)
            first_tile = _dot(lower, jnp.broadcast_to(tiles, (N_EXPERTS, LANES)).astype(BF16))
            base_ref[...] = first_tile[:, :1] * tm
        r = lax.broadcasted_iota(I32, (tc, tc), 0)
        c = lax.broadcasted_iota(I32, (tc, tc), 1)
        incl = _dot(sel.astype(BF16), (r <= c).astype(BF16))
        row = (base_ref[...] + incl - 1.0).reshape(eid.shape)
        base_ref[...] += jnp.sum(sel, axis=1, keepdims=True)
        pos, wts = [], []
        for pick in picks:
            hit = eid == pick
            pos.append(_sum_experts(jnp.where(hit, row, 0.0)).reshape(1, tc))
            wts.append(_sum_experts(jnp.where(hit, comb, 0.0)).reshape(1, tc))
        pos_ref[...] = jnp.concatenate(pos, axis=0).astype(I32)
        wts = jnp.concatenate(wts + [jnp.zeros((LANES - TOP_K, tc), F32)], axis=0).T
        w_ref[...] = _rows_to_slab(jnp.concatenate(
            [jnp.broadcast_to(wts[:, k:k + 1], (tc, LANES)) for k in range(TOP_K)], axis=-1))


def _router(logits_ctx, logits_lat, b_router, layer, tm, tc=1024):
    e = logits_ctx.shape[0]
    n_ctx = logits_ctx.shape[1] // tc
    t = logits_ctx.shape[1] + logits_lat.shape[1]
    return pl.pallas_call(
        functools.partial(_router_kernel, tm=tm, n_ctx=n_ctx),
        grid=(2, t // tc),
        in_specs=_two_part_specs((e, tc), n_ctx, axis=1) + [
            pl.BlockSpec((None, e, 1), lambda p, i: (layer, 0, 0))],
        out_specs=[pl.BlockSpec((TOP_K, tc), lambda p, i: (0, p * i)),
                   pl.BlockSpec((tc, TOP_K, LANES), lambda p, i: (p * i, 0, 0)),
                   pl.BlockSpec((e, LANES), lambda p, i: (0, 0))],
        out_shape=[jax.ShapeDtypeStruct((TOP_K, t), I32), jax.ShapeDtypeStruct((t, TOP_K, LANES), F32),
                   jax.ShapeDtypeStruct((e, LANES), F32)],
        scratch_shapes=[pltpu.VMEM((e, 1), F32), pltpu.VMEM((e, 1), F32)],
        compiler_params=_cparams(("arbitrary", "arbitrary")),
        name="router",
    )(logits_ctx, logits_lat, b_router.reshape(-1, e, 1))


def _tile_plan(counts, tm, n_tiles):
    counts = counts.astype(I32)
    tiles = (counts + (tm - 1)) // tm
    ends = jnp.cumsum(tiles)
    n_active = ends[-1]
    tile = jnp.arange(n_tiles, dtype=I32)
    expert = jnp.sum((tile[:, None] >= ends[None, :]).astype(I32), axis=1)
    expert = jnp.where(tile < n_active, expert, expert[n_active - 1])
    return counts, (ends - tiles) * tm, tiles, expert, n_active.reshape(1)


def _dispatch_kernel(row0_ref, tiles_ref, cnt_ref, hc_ref, hl_ref, pos_ref, xs_ref, h_ref, zero_ref, pos_smem,
                     row_sem, zero_sem, pos_sem, *, tm, n_ctx):
    tt = h_ref.shape[0]
    h = jnp.where(pl.program_id(0) < n_ctx, hc_ref[...], hl_ref[...])
    h_ref[...] = _rows_to_slab(h)

    def zero_block(e, b, action):
        start = row0_ref[e] + (tiles_ref[e] - 1) * tm + b * ZERO_ROWS

        @pl.when(jnp.logical_and(tiles_ref[e] > 0, start + ZERO_ROWS > row0_ref[e] + cnt_ref[e]))
        def _():
            action(pltpu.make_async_copy(
                zero_ref, xs_ref.at[pl.ds(pl.multiple_of(start, ZERO_ROWS), ZERO_ROWS)], zero_sem))

    def for_all_zero_blocks(action):
        def per_expert(e, carry):
            for b in range(tm // ZERO_ROWS):
                zero_block(e, b, action)
            return carry
        lax.fori_loop(0, N_EXPERTS, per_expert, 0)

    @pl.when(pl.program_id(0) == 0)
    def _zero_padding_rows():
        zero_ref[...] = jnp.zeros_like(zero_ref)
        for_all_zero_blocks(lambda copy: copy.start())
        for_all_zero_blocks(lambda copy: copy.wait())

    pos_copy = pltpu.make_async_copy(pos_ref, pos_smem, pos_sem)
    pos_copy.start()
    pos_copy.wait()

    def issue(t, carry):
        for k in range(TOP_K):
            pltpu.make_async_copy(h_ref.at[t], xs_ref.at[pos_smem[k, t]],
                                  row_sem).start(priority=k % N_DMA_PRIORITIES)
        return carry

    lax.fori_loop(0, tt, issue, 0, unroll=8)
    for _ in range(TOP_K):
        pltpu.make_async_copy(h_ref, xs_ref.at[pl.ds(0, tt)], row_sem).wait()


def _dispatch(h_ctx, h_lat, pos, first_row, tiles, counts, n_rows, tm, tt=DISPATCH_TT):
    d = h_ctx.shape[1]
    n_ctx = h_ctx.shape[0] // tt
    t = h_ctx.shape[0] + h_lat.shape[0]
    assert tm % ZERO_ROWS == 0
    return pl.pallas_call(
        functools.partial(_dispatch_kernel, tm=tm, n_ctx=n_ctx),
        grid_spec=pltpu.PrefetchScalarGridSpec(
            num_scalar_prefetch=3,
            grid=(t // tt,),
            in_specs=_two_part_specs((tt, d), n_ctx, axis=0, rest=1) + [
                pl.BlockSpec((TOP_K, tt), lambda i, *_: (0, i))],
            out_specs=pl.BlockSpec(memory_space=pl.ANY),
            scratch_shapes=[pltpu.VMEM((tt, d // LANES, LANES), BF16),
                            pltpu.VMEM((ZERO_ROWS, d // LANES, LANES), BF16),
                            pltpu.SMEM((TOP_K, tt), I32),
                            pltpu.SemaphoreType.DMA, pltpu.SemaphoreType.DMA, pltpu.SemaphoreType.DMA],
        ),
        out_shape=jax.ShapeDtypeStruct((n_rows, d // LANES, LANES), BF16),
        compiler_params=_cparams(("arbitrary",)),
        name="moe_dispatch",
    )(first_row, tiles, counts, h_ctx, h_lat, pos)


def _expert_kernel(te_ref, na_ref, xs_ref, wg_ref, wu_ref, wd_ref, o_ref, wgu_bf, wd_bf):
    i = pl.program_id(0)
    f = wg_ref.shape[1]

    @pl.when(i < na_ref[0])
    def _():
        @pl.when(jnp.logical_or(i == 0, te_ref[i] != te_ref[jnp.maximum(i - 1, 0)]))
        def _():
            wgu_bf[:, :f] = wg_ref[...].astype(BF16)
            wgu_bf[:, f:] = wu_ref[...].astype(BF16)
            wd_bf[...] = wd_ref[...].astype(BF16)

        x = _slab_to_rows(xs_ref[...])
        hgu = _dot(x, wgu_bf[...])
        out = _dot((_silu(hgu[:, :f]) * hgu[:, f:]).astype(BF16), wd_bf[...])
        o_ref[...] = _rows_to_slab(out.astype(BF16))


def _experts(xs, tile_expert, n_active, w_gate, w_up, w_down, layer, tm):
    n_rows, s, _ = xs.shape
    d, f = w_gate.shape[-2:]
    row = lambda i, te, na: (jnp.minimum(i, na[0] - 1), 0, 0)
    wspec = lambda shape: pl.BlockSpec((None, None) + shape, lambda i, te, na: (layer, te[i], 0, 0))
    return pl.pallas_call(
        _expert_kernel,
        grid_spec=pltpu.PrefetchScalarGridSpec(
            num_scalar_prefetch=2,
            grid=(n_rows // tm,),
            in_specs=[pl.BlockSpec((tm, s, LANES), row), wspec((d, f)), wspec((d, f)), wspec((f, d))],
            out_specs=pl.BlockSpec((tm, s, LANES), row),
            scratch_shapes=[pltpu.VMEM((d, 2 * f), BF16), pltpu.VMEM((f, d), BF16)],
        ),
        out_shape=jax.ShapeDtypeStruct((n_rows, s, LANES), BF16),
        compiler_params=_cparams(("arbitrary",)),
        name="moe_experts",
    )(tile_expert, n_active, xs, w_gate, w_up, w_down)


def _combine_kernel(pos_ref, w_ref, h_ref, x_ref, gt2_ref, sg_ref, su_ref, sd_ref, gfin_ref, ys_ref,
                    o_ref, buf_ref, pos_smem, row_sem, pos_sem, *, final):
    tt = x_ref.shape[0]
    pos_copy = pltpu.make_async_copy(pos_ref, pos_smem, pos_sem)
    pos_copy.start()
    pos_copy.wait()

    def issue(t, carry):
        for k in range(TOP_K):
            pltpu.make_async_copy(ys_ref.at[pos_smem[k, t]], buf_ref.at[k, t],
                                  row_sem).start(priority=k % N_DMA_PRIORITIES)
        return carry

    lax.fori_loop(0, tt, issue, 0, unroll=8)
    h = h_ref[...]
    shared = _dot((_silu(_dot(h, sg_ref[...])) * _dot(h, su_ref[...])).astype(BF16), sd_ref[...])
    for k in range(TOP_K):
        pltpu.make_async_copy(ys_ref.at[pl.ds(0, tt)], buf_ref.at[k], row_sem).wait()
    routed = buf_ref[0].astype(F32) * w_ref[:, 0:1, :]
    for k in range(1, TOP_K):
        routed = routed + buf_ref[k].astype(F32) * w_ref[:, k:k + 1, :]
    y = x_ref[...] + gt2_ref[...] * (_slab_to_rows(routed) + shared)
    o_ref[...] = _rms(y, gfin_ref[...]) if final else y


def _combine(pos, w, h2, x, mod5, ys, p, g_final, layer, row_fn, row0, final, tt=COMBINE_TT):
    t, d = x.shape
    f = p["ws_gate"].shape[-1]
    tile0 = row0 // tt
    tok = lambda i: (i, 0)
    lyr = lambda shape: _resident((None,) + shape, lambda i: (layer,) + (0,) * len(shape))
    return pl.pallas_call(
        functools.partial(_combine_kernel, final=final),
        grid=(t // tt,),
        in_specs=[
            pl.BlockSpec((TOP_K, tt), lambda i: (0, tile0 + i)),
            pl.BlockSpec((tt, TOP_K, LANES), lambda i: (tile0 + i, 0, 0)),
            pl.BlockSpec((tt, d), tok), pl.BlockSpec((tt, d), tok),
            pl.BlockSpec((None, None, None, 1, d), lambda i: (layer, row_fn(i), 5, 0, 0)),
            lyr((d, f)), lyr((d, f)), lyr((f, d)),
            _resident((1, d), lambda i: (0, 0)),
            pl.BlockSpec(memory_space=pl.ANY),
        ],
        out_specs=pl.BlockSpec((tt, d), tok),
        out_shape=jax.ShapeDtypeStruct((t, d), F32),
        scratch_shapes=[pltpu.VMEM((TOP_K, tt, d // LANES, LANES), BF16), pltpu.SMEM((TOP_K, tt), I32),
                        pltpu.SemaphoreType.DMA, pltpu.SemaphoreType.DMA],
        compiler_params=_cparams(("arbitrary",)),
        name="moe_combine",
    )(pos, w, h2, x, mod5, p["ws_gate"], p["ws_up"], p["ws_down"], g_final.reshape(1, d), ys)


def _rope_tables(n_tokens, dim):
    rows = n_tokens // GRID_W
    row = jnp.repeat(jnp.arange(rows, dtype=jnp.int32), GRID_W).astype(F32)
    col = jnp.tile(jnp.arange(GRID_W, dtype=jnp.int32), rows).astype(F32)
    half = dim // 2
    inv_freq = ROPE_THETA ** (-(jnp.arange(half // 2, dtype=F32) * 2.0 / half))
    ang_r = row[:, None] * inv_freq[None, :]
    ang_c = col[:, None] * inv_freq[None, :]
    cos = jnp.concatenate([jnp.cos(ang_r)] * 2 + [jnp.cos(ang_c)] * 2, axis=-1)
    sin = jnp.concatenate([-jnp.sin(ang_r), jnp.sin(ang_r), -jnp.sin(ang_c), jnp.sin(ang_c)], axis=-1)
    pad = ((0, 0), (0, LANES - dim))
    return jnp.pad(cos, pad), jnp.pad(sin, pad)


def _prep_params(w_in, w_uq, w_ukv, w_out, w_router, w_gate, w_up, w_down, ws_gate, ws_up, ws_down):
    depth = w_in.shape[0]
    w_in_p = jnp.pad(w_in.astype(BF16), ((0, 0), (0, 0), (0, W_IN_PAD - w_in.shape[-1])))
    w_uq_r = jnp.pad(w_uq.reshape(depth, MLA_Q_RANK, MLA_HEADS, MLA_NOPE + MLA_ROPE),
                     ((0, 0), (0, 0), (0, 0), (0, MLA_QH - MLA_NOPE - MLA_ROPE)))
    w_uq_r = w_uq_r.reshape(depth, MLA_Q_RANK, MLA_HEADS * MLA_QH).astype(BF16)
    return {
        "w_in": w_in_p, "w_uq": w_uq_r, "w_ukv": w_ukv.astype(BF16), "w_out": w_out.astype(BF16),
        "w_router_t": jnp.swapaxes(w_router, 1, 2),
        "w_gate": w_gate, "w_up": w_up, "w_down": w_down,
        "ws_gate": ws_gate.astype(BF16), "ws_up": ws_up.astype(BF16), "ws_down": ws_down.astype(BF16),
    }


def kernel(x_prompt, x_sample, c, cache_gqa_k, cache_gqa_v, cache_mla_ckv, cache_mla_kpe, c_ctx, w_mod, b_mod, g_mix, w_in, g_q, g_k, conv_w, g_mla_q, g_mla_kv, w_uq, w_ukv, g_grp, w_out, g_ffn, w_router, b_router, w_gate, w_up, w_down, ws_gate, ws_up, ws_down, g_final):
    batch, seq, d = x_prompt.shape
    dec_batch, dec_seq, _ = x_sample.shape
    depth = w_mod.shape[0]
    past = cache_gqa_k.shape[2]
    t_ctx, t_lat = batch * seq, dec_batch * dec_seq
    n_moe_tiles = pl.cdiv((t_ctx + t_lat) * TOP_K, MOE_TM) + N_EXPERTS

    p = _prep_params(w_in, w_uq, w_ukv, w_out, w_router, w_gate, w_up, w_down, ws_gate, ws_up, ws_down)
    for name, val in (("g_q", g_q), ("g_k", g_k), ("g_mla_q", g_mla_q), ("g_mla_kv", g_mla_kv),
                      ("g_grp", g_grp), ("g_ffn", g_ffn)):
        p[name] = val.reshape(depth, 1, -1)
    p["conv_w"] = conv_w

    mod_rows = 8
    cond = jnp.zeros((mod_rows, d), F32).at[0].set(c_ctx).at[1:1 + dec_batch].set(c)
    mod5 = _modulation(cond, w_mod, b_mod).reshape(depth, mod_rows, 6, 1, d)
    ctx_row = lambda i: 0
    lat_row = lambda tile: (lambda i: 1 + i // (dec_seq // tile))

    cache = (cache_gqa_k.reshape(dec_batch, depth, past, KV_W),
             cache_gqa_v.reshape(dec_batch, depth, past, KV_W), cache_mla_ckv, cache_mla_kpe)
    rope = _rope_tables(dec_seq, HEAD_DIM) + _rope_tables(dec_seq, MLA_ROPE)

    xc, xl = x_prompt.reshape(t_ctx, d), x_sample.reshape(t_lat, d)
    new_ctx = []
    for layer in range(depth):
        final = layer == depth - 1
        zqc, zsc = _in_proj(xc, mod5, g_mix, p["w_in"], layer, ctx_row, IN_TM)
        zql, zsl = _in_proj(xl, mod5, g_mix, p["w_in"], layer, lat_row(IN_TM), IN_TM)
        xc, h2c, lgc, nk, nv, nckv, nkpe = _mixer(
            zqc, zsc, xc, mod5, p, layer, n_seq=batch, seq=seq, mod_row_fn=ctx_row)
        xl, h2l, lgl = _mixer(
            zql, zsl, xl, mod5, p, layer, n_seq=dec_batch, seq=dec_seq,
            mod_row_fn=lambda b: 1 + b, cache=cache, rope=rope, qb=128)
        new_ctx.append((nk, nv, nckv, nkpe))
        pos, w_tok, counts = _router(lgc, lgl, b_router, layer, MOE_TM)
        counts, first_row, tiles, tile_expert, n_active = _tile_plan(counts[:, 0], MOE_TM, n_moe_tiles)
        xs = _dispatch(h2c, h2l, pos, first_row, tiles, counts, n_moe_tiles * MOE_TM, MOE_TM)
        ys = _experts(xs, tile_expert, n_active, p["w_gate"], p["w_up"], p["w_down"], layer, MOE_TM)
        xc = _combine(pos, w_tok, h2c, xc, mod5, ys, p, g_final, layer, ctx_row, 0, final)
        xl = _combine(pos, w_tok, h2l, xl, mod5, ys, p, g_final, layer, lat_row(COMBINE_TT), t_ctx, final)
    stack = lambda i: jnp.stack([lc[i] for lc in new_ctx], axis=1)
    new_k = stack(0).reshape(batch, depth, seq, GQA_KV_HEADS, HEAD_DIM)
    new_v = stack(1).reshape(batch, depth, seq, GQA_KV_HEADS, HEAD_DIM)
    return (xc.reshape(batch, seq, d), xl.reshape(dec_batch, dec_seq, d), new_k, new_v, stack(2), stack(3))
```

```python
import functools

import jax
import jax.numpy as jnp
from jax import lax
from jax.experimental import pallas as pl
from jax.experimental.pallas import tpu as pltpu

F32 = jnp.float32
BF16 = jnp.bfloat16
I32 = jnp.int32

EPS = 1e-6
ROPE_THETA = 10000.0
GRID_W = 64
GQA_HEADS = 6
GQA_KV_HEADS = 2
HEAD_DIM = 128
CONV_CH = 512
MLA_HEADS = 6
MLA_Q_RANK = 512
MLA_KV_RANK = 256
MLA_NOPE = 128
MLA_ROPE = 64
MLA_V = 128
GQA_WIDTH = GQA_HEADS * HEAD_DIM
MLA_WIDTH = MLA_HEADS * MLA_V
N_EXPERTS = 64
TOP_K = 8
N_GROUPS = 8
TOPK_GROUPS = 4
GROUP_SIZE = N_EXPERTS // N_GROUPS
ROUTED_SCALE = 2.5
MOE_TM = 704
IN_TM = 512
COMBINE_TT = 256
DISPATCH_TT = 1024
ZERO_ROWS = 64
LANES = 128
SUBLANES = 8
N_DMA_PRIORITIES = 2
VMEM_LIMIT_BYTES = 56 * 1024 * 1024

ZQ_W = GQA_WIDTH + MLA_Q_RANK + CONV_CH
KV_W = GQA_KV_HEADS * HEAD_DIM
ZS_W = 2 * KV_W + MLA_KV_RANK + 2 * CONV_CH + LANES
ZS_K, ZS_V, ZS_CKV = 0, KV_W, 2 * KV_W
ZS_CC = ZS_CKV + MLA_KV_RANK
ZS_CH = ZS_CC + CONV_CH
ZS_KR = ZS_CH + CONV_CH
_IN_Q, _IN_K, _IN_V, _IN_CB, _IN_CC, _IN_CH, _IN_CQ, _IN_CKV, _IN_KR = (
    0, 768, 1024, 1280, 1792, 2304, 2816, 3328, 3584)
W_IN_PAD = _IN_KR + LANES
ZQ_PIECES = ((_IN_Q, GQA_WIDTH), (_IN_CQ, MLA_Q_RANK), (_IN_CB, CONV_CH))
ZS_PIECES = ((_IN_K, 2 * KV_W), (_IN_CKV, MLA_KV_RANK), (_IN_CC, 2 * CONV_CH), (_IN_KR, LANES))
CONV_HALO = 8
MLA_QH = 2 * LANES
MLA_KVH = MLA_NOPE + MLA_V


def _cparams(sem, vmem=VMEM_LIMIT_BYTES):
    return pltpu.CompilerParams(dimension_semantics=sem, vmem_limit_bytes=vmem)


def _resident(shape, index_map):
    return pl.BlockSpec(shape, index_map, pipeline_mode=pl.Buffered(1))


def _rms(x, g):
    ms = jnp.mean(x * x, axis=-1, keepdims=True)
    return x * lax.rsqrt(ms + EPS) * g


def _dot(a, b):
    return jnp.dot(a, b, preferred_element_type=F32)


def _dot_nt(a, b):
    return lax.dot_general(a, b, (((1,), (1,)), ((), ())), preferred_element_type=F32)


def _rows_per_vreg(dtype):
    return SUBLANES * 4 // jnp.dtype(dtype).itemsize


def _rows_to_slab(x):
    n, d = x.shape
    s, g = d // LANES, _rows_per_vreg(x.dtype)
    y = jnp.stack([x[:, LANES * j:LANES * (j + 1)].reshape(n // g, g, LANES) for j in range(s)], axis=1)
    return jnp.swapaxes(y, 1, 2).reshape(n, s, LANES)


def _slab_to_rows(x3):
    n, s, _ = x3.shape
    g = _rows_per_vreg(x3.dtype)
    y = jnp.swapaxes(x3.reshape(n // g, g, s, LANES), 1, 2)
    return jnp.concatenate([y[:, j].reshape(n, LANES) for j in range(s)], axis=-1)


def _silu(x):
    return x * jax.nn.sigmoid(x)


def _rope(x, cos, sin_signed, hb):
    lane = lax.broadcasted_iota(jnp.int32, x.shape, 1)
    partner = jnp.where((lane % (2 * hb)) < hb,
                        pltpu.roll(x, LANES - hb, axis=1), pltpu.roll(x, hb, axis=1))
    return x * cos + partner * sin_signed


def _mod_kernel(c_ref, w_ref, b_ref, o_ref):
    c = c_ref[...]
    a = (c * jax.nn.sigmoid(c)).astype(BF16)
    o_ref[...] = _dot(a, w_ref[...].astype(BF16)) + b_ref[...]


def _modulation(cond, w_mod, b_mod, tn=1024):
    depth, d, n = w_mod.shape
    rows = cond.shape[0]
    return pl.pallas_call(
        _mod_kernel,
        grid=(depth, n // tn),
        in_specs=[
            pl.BlockSpec((rows, d), lambda l, j: (0, 0)),
            pl.BlockSpec((None, d, tn), lambda l, j: (l, 0, j)),
            pl.BlockSpec((None, 1, tn), lambda l, j: (l, 0, j)),
        ],
        out_specs=pl.BlockSpec((None, rows, tn), lambda l, j: (l, 0, j)),
        out_shape=jax.ShapeDtypeStruct((depth, rows, n), F32),
        compiler_params=_cparams(("parallel", "parallel")),
        name="modulation",
    )(cond, w_mod, b_mod.reshape(depth, 1, n))


def _in_proj_kernel(x_ref, g_ref, sh_ref, sc_ref, w_ref, zq_ref, zs_ref):
    h = _rms(x_ref[...], g_ref[...]) * (1.0 + sc_ref[...]) + sh_ref[...]
    z = _dot(h.astype(BF16), w_ref[...])
    for ref, pieces in ((zq_ref, ZQ_PIECES), (zs_ref, ZS_PIECES)):
        dst = 0
        for src, width in pieces:
            ref[:, dst:dst + width] = z[:, src:src + width]
            dst += width


def _in_proj(x, mod5, g_mix, w_in, layer, row_fn, tm):
    t, d = x.shape
    n = w_in.shape[-1]
    mspec = lambda chunk: pl.BlockSpec((None, None, None, 1, d),
                                       lambda i: (layer, row_fn(i), chunk, 0, 0))
    return pl.pallas_call(
        _in_proj_kernel,
        grid=(t // tm,),
        in_specs=[
            pl.BlockSpec((tm, d), lambda i: (i, 0)),
            pl.BlockSpec((None, 1, d), lambda i: (layer, 0, 0)),
            mspec(0), mspec(1),
            _resident((None, d, n), lambda i: (layer, 0, 0)),
        ],
        out_specs=[pl.BlockSpec((tm, ZQ_W), lambda i: (i, 0)),
                   pl.BlockSpec((tm, ZS_W), lambda i: (i, 0))],
        out_shape=[jax.ShapeDtypeStruct((t, ZQ_W), F32), jax.ShapeDtypeStruct((t, ZS_W), F32)],
        compiler_params=_cparams(("parallel",)),
        name="in_proj",
    )(x, g_mix.reshape(-1, 1, d), mod5, mod5, w_in)


def _mixer_kernel(*refs, seq, past, qb, latent):
    it = iter(refs)
    zq_ref, zs_ref, x_ref = next(it), next(it), next(it)
    gt1_ref, sh2_ref, sc2_ref = next(it), next(it), next(it)
    gq_ref, gk_ref, gmq_ref, gmkv_ref = next(it), next(it), next(it), next(it)
    convw_ref, ggrp_ref, gffn_ref = next(it), next(it), next(it)
    wuq_ref, wukv_ref, wout_ref, wrt_ref = next(it), next(it), next(it), next(it)
    if latent:
        ck_ref, cv_ref, cckv_ref, ckpe_ref = next(it), next(it), next(it), next(it)
        cosa_ref, sina_ref, cosc_ref, sinc_ref = next(it), next(it), next(it), next(it)
    xo_ref, h2_ref, lg_ref = next(it), next(it), next(it)
    if not latent:
        nk_ref, nv_ref, nckv_ref, nkpe_ref = next(it), next(it), next(it), next(it)
    kbf_ref, vbf_ref, kvm_ref, kpe_ref, conv_ref, u_ref = (next(it) for _ in range(6))

    j = pl.program_id(1)

    single = seq == qb and past == 0
    held = {}

    def keep(name, ref, index, value):
        if single:
            held[name] = held.get(name, ()) + (value,)
        else:
            ref[index] = value

    def _per_sequence():
        if latent:
            kbf_ref[:past, :] = ck_ref[...].astype(BF16)
            vbf_ref[:past, :] = cv_ref[...].astype(BF16)
            kpe_ref[:past, :MLA_ROPE] = ckpe_ref[...].astype(BF16)
            kpe_ref[:past, MLA_ROPE:] = jnp.zeros((past, LANES - MLA_ROPE), BF16)
            for c0 in range(0, past, qb):
                c1 = min(c0 + qb, past)
                kvm_ref[c0:c1, :] = _dot(cckv_ref[c0:c1, :].astype(BF16), wukv_ref[...]).astype(BF16)
        u_ref[:CONV_HALO, :] = jnp.zeros((CONV_HALO, CONV_CH), F32)
        u_ref[CONV_HALO + seq:, :] = jnp.zeros((CONV_HALO, CONV_CH), F32)
        for c0 in range(0, seq, qb):
            rows, prow = slice(c0, c0 + qb), slice(past + c0, past + c0 + qb)
            for hk in range(GQA_KV_HEADS):
                sl = slice(hk * HEAD_DIM, (hk + 1) * HEAD_DIM)
                k = _rms(zs_ref[rows, ZS_K + hk * HEAD_DIM:ZS_K + (hk + 1) * HEAD_DIM], gk_ref[...])
                if latent:
                    k = _rope(k, cosa_ref[rows, :], sina_ref[rows, :], HEAD_DIM // 4)
                else:
                    nk_ref[rows, sl] = k
                keep("k", kbf_ref, (prow, sl), k.astype(BF16))
            v = zs_ref[rows, ZS_V:ZS_V + KV_W]
            keep("v", vbf_ref, (prow, slice(None)), v.astype(BF16))
            ckv_n = _rms(zs_ref[rows, ZS_CKV:ZS_CKV + MLA_KV_RANK], gmkv_ref[...])
            kpe = zs_ref[rows, ZS_KR:ZS_KR + LANES]
            if latent:
                kpe = _rope(kpe, cosc_ref[rows, :], sinc_ref[rows, :], MLA_ROPE // 4)
            else:
                nv_ref[rows, :] = v
                nckv_ref[rows, :] = ckv_n
                nkpe_ref[rows, :] = kpe[:, :MLA_ROPE]
            keep("kpe", kpe_ref, (prow, slice(None)), kpe.astype(BF16))
            keep("kvm", kvm_ref, (prow, slice(None)), _dot(ckv_n.astype(BF16), wukv_ref[...]).astype(BF16))
            u_ref[CONV_HALO + c0:CONV_HALO + c0 + qb, :] = (
                zs_ref[rows, ZS_CC:ZS_CC + CONV_CH] * zs_ref[rows, ZS_CH:ZS_CH + CONV_CH])
        for c0 in range(0, seq, qb):
            taps = [u_ref[CONV_HALO - 1 + c0 + i:CONV_HALO - 1 + c0 + i + qb, :] * convw_ref[i:i + 1, :]
                    for i in range(3)]
            keep("conv", conv_ref, (slice(c0, c0 + qb), slice(None)), taps[0] + taps[1] + taps[2])

    r0 = pl.multiple_of(j * qb, qb)
    if single:
        _per_sequence()
        k_all, v_all = jnp.concatenate(held["k"], axis=-1), held["v"][0]
        kvm_all, kpe_all, conv_q = held["kvm"][0], held["kpe"][0], held["conv"][0]
    else:
        pl.when(j == 0)(_per_sequence)
        k_all, v_all, kvm_all, kpe_all = kbf_ref, vbf_ref, kvm_ref, kpe_ref
        conv_q = conv_ref[pl.ds(r0, qb), :]

    def attend(s, v_bf):
        m = jnp.max(s, axis=-1, keepdims=True)
        e = jnp.exp(s - m)
        return _dot(e.astype(BF16), v_bf) / jnp.sum(e, axis=-1, keepdims=True)

    group = GQA_HEADS // GQA_KV_HEADS
    outs_a = []
    for hk in range(GQA_KV_HEADS):
        qs = []
        for h in range(hk * group, (hk + 1) * group):
            q = _rms(zq_ref[:, h * HEAD_DIM:(h + 1) * HEAD_DIM], gq_ref[...])
            if latent:
                q = _rope(q, cosa_ref[pl.ds(r0, qb), :], sina_ref[pl.ds(r0, qb), :], HEAD_DIM // 4)
            qs.append(q.astype(BF16))
        ksl = slice(hk * HEAD_DIM, (hk + 1) * HEAD_DIM)
        s = _dot_nt(jnp.concatenate(qs, axis=0), k_all[:, ksl]) * (HEAD_DIM ** -0.5)
        o = attend(s, v_all[:, ksl])
        outs_a += [o[g * qb:(g + 1) * qb] for g in range(group)]
    out_a = jnp.concatenate(outs_a, axis=-1)

    cq_n = _rms(zq_ref[:, GQA_WIDTH:GQA_WIDTH + MLA_Q_RANK], gmq_ref[...])
    q_m = _dot(cq_n.astype(BF16), wuq_ref[...])
    outs_c = []
    for h in range(MLA_HEADS):
        q_nope = q_m[:, h * MLA_QH:h * MLA_QH + MLA_NOPE]
        q_pe = q_m[:, h * MLA_QH + MLA_NOPE:(h + 1) * MLA_QH]
        if latent:
            q_pe = _rope(q_pe, cosc_ref[pl.ds(r0, qb), :], sinc_ref[pl.ds(r0, qb), :], MLA_ROPE // 4)
        q_h = jnp.concatenate([q_nope.astype(BF16), q_pe.astype(BF16)], axis=-1)
        k_h = jnp.concatenate([kvm_all[:, h * MLA_KVH:h * MLA_KVH + MLA_NOPE], kpe_all[...]], axis=-1)
        s = _dot_nt(q_h, k_h) * ((MLA_NOPE + MLA_ROPE) ** -0.5)
        outs_c.append(attend(s, kvm_all[:, h * MLA_KVH + MLA_NOPE:(h + 1) * MLA_KVH]))
    out_c = jnp.concatenate(outs_c, axis=-1)

    out_b = zq_ref[:, GQA_WIDTH + MLA_Q_RANK:] * conv_q

    g = ggrp_ref
    merged = jnp.concatenate([
        _rms(out_a, g[:, :GQA_WIDTH]).astype(BF16),
        _rms(out_b, g[:, GQA_WIDTH:GQA_WIDTH + CONV_CH]).astype(BF16),
        _rms(out_c, g[:, GQA_WIDTH + CONV_CH:]).astype(BF16)], axis=-1)
    x_new = x_ref[...] + gt1_ref[...] * _dot(merged, wout_ref[...])
    xo_ref[...] = x_new
    h2 = _rms(x_new, gffn_ref[...]) * (1.0 + sc2_ref[...]) + sh2_ref[...]
    h2_ref[...] = h2.astype(BF16)
    def split(a):
        hi = a.astype(BF16)
        return hi, (a - hi.astype(F32)).astype(BF16)

    (w_hi, w_lo), (h_hi, h_lo) = split(wrt_ref[...]), split(h2)
    lg_ref[...] = _dot_nt(w_hi, h_hi) + (_dot_nt(w_hi, h_lo) + _dot_nt(w_lo, h_hi))


def _mixer(zq, zs, x, mod5, p, layer, *, n_seq, seq, mod_row_fn, cache=None, rope=None, qb=256):
    d = x.shape[1]
    t_out = n_seq * seq
    latent = cache is not None
    past = cache[0].shape[2] if latent else 0
    nq = seq // qb
    sk = past + seq

    def const(shape):
        return _resident(shape, lambda b, j: (0,) * len(shape))

    def lyr(shape):
        return _resident((None,) + shape, lambda b, j: (layer,) + (0,) * len(shape))

    mspec = lambda chunk: pl.BlockSpec((None, None, None, 1, d),
                                       lambda b, j: (layer, mod_row_fn(b), chunk, 0, 0))
    qrow = orow = lambda b, j: (b * nq + j, 0)
    seq_spec = _resident if nq > 1 else pl.BlockSpec
    in_specs = [
        pl.BlockSpec((qb, ZQ_W), qrow),
        seq_spec((seq, ZS_W), lambda b, j: (b, 0)),
        pl.BlockSpec((qb, d), qrow),
        mspec(2), mspec(3), mspec(4),
        lyr((1, HEAD_DIM)), lyr((1, HEAD_DIM)), lyr((1, MLA_Q_RANK)), lyr((1, MLA_KV_RANK)),
        lyr((3, CONV_CH)), lyr((1, d)), lyr((1, d)),
        lyr((MLA_Q_RANK, MLA_HEADS * MLA_QH)), lyr((MLA_KV_RANK, MLA_HEADS * MLA_KVH)),
        lyr((d, d)), lyr((N_EXPERTS, d)),
    ]
    args = [zq, zs, x, mod5, mod5, mod5,
            p["g_q"], p["g_k"], p["g_mla_q"], p["g_mla_kv"], p["conv_w"], p["g_grp"], p["g_ffn"],
            p["w_uq"], p["w_ukv"], p["w_out"], p["w_router_t"]]
    if latent:
        cspec = lambda w: pl.BlockSpec((None, None, past, w), lambda b, j: (b, layer, 0, 0))
        in_specs += [cspec(KV_W), cspec(KV_W), cspec(MLA_KV_RANK), cspec(MLA_ROPE)]
        in_specs += [const((seq, LANES))] * 4
        args += list(cache) + list(rope)
    out_specs = [pl.BlockSpec((qb, d), orow), pl.BlockSpec((qb, d), orow),
                 pl.BlockSpec((N_EXPERTS, qb), lambda b, j: (0, b * nq + j))]
    out_shape = [jax.ShapeDtypeStruct((t_out, d), F32), jax.ShapeDtypeStruct((t_out, d), BF16),
                 jax.ShapeDtypeStruct((N_EXPERTS, t_out), F32)]
    if not latent:
        sspec = lambda w: pl.BlockSpec((None, seq, w), lambda b, j: (b, 0, 0))
        out_specs += [sspec(KV_W), sspec(KV_W), sspec(MLA_KV_RANK), sspec(MLA_ROPE)]
        out_shape += [jax.ShapeDtypeStruct((n_seq, seq, w), F32)
                      for w in (KV_W, KV_W, MLA_KV_RANK, MLA_ROPE)]
    scratch = [pltpu.VMEM((sk, KV_W), BF16), pltpu.VMEM((sk, KV_W), BF16),
               pltpu.VMEM((sk, MLA_HEADS * MLA_KVH), BF16), pltpu.VMEM((sk, LANES), BF16),
               pltpu.VMEM((seq, CONV_CH), F32), pltpu.VMEM((seq + 2 * CONV_HALO, CONV_CH), F32)]
    return pl.pallas_call(
        functools.partial(_mixer_kernel, seq=seq, past=past, qb=qb, latent=latent),
        grid=(n_seq, nq),
        in_specs=in_specs, out_specs=out_specs, out_shape=out_shape, scratch_shapes=scratch,
        compiler_params=_cparams(("parallel", "arbitrary")),
        name="mixer_latent" if latent else "mixer_context",
    )(*args)


def _route(logits, bias):
    tc = logits.shape[1]
    shape3 = (N_GROUPS, GROUP_SIZE, tc)
    scores = jax.nn.sigmoid(logits)
    choice = (scores + bias).reshape(shape3)
    scores = scores.reshape(shape3)
    neg = -jnp.inf
    member = lax.broadcasted_iota(jnp.int32, shape3, 1)
    m1 = jnp.max(choice, axis=1, keepdims=True)
    first = jnp.min(jnp.where(choice == m1, member, GROUP_SIZE), axis=1, keepdims=True)
    m2 = jnp.max(jnp.where(member == first, neg, choice), axis=1, keepdims=True)
    gscore = m1 + m2
    gid = lax.broadcasted_iota(jnp.int32, gscore.shape, 0)
    gmask = jnp.zeros(gscore.shape, jnp.int32)
    for _ in range(TOPK_GROUPS):
        m = jnp.max(gscore, axis=0, keepdims=True)
        pick = jnp.min(jnp.where(gscore == m, gid, N_GROUPS), axis=0, keepdims=True)
        hit = gid == pick
        gmask = jnp.where(hit, 1, gmask)
        gscore = jnp.where(hit, neg, gscore)
    eid = lax.broadcasted_iota(jnp.int32, shape3, 0) * GROUP_SIZE + member
    cur = jnp.where(gmask > 0, choice, neg)
    sel = jnp.zeros(shape3, jnp.int32)
    picks = []
    for _ in range(TOP_K):
        m = jnp.max(jnp.max(cur, axis=0, keepdims=True), axis=1, keepdims=True)
        cand = jnp.where(cur == m, eid, N_EXPERTS)
        pick = jnp.min(jnp.min(cand, axis=0, keepdims=True), axis=1, keepdims=True)
        hit = eid == pick
        sel = jnp.where(hit, 1, sel)
        cur = jnp.where(hit, neg, cur)
        picks.append(pick)
    w = jnp.where(sel > 0, scores, 0.0)
    wsum = jnp.sum(jnp.sum(w, axis=0, keepdims=True), axis=1, keepdims=True)
    return eid, picks, sel, w / wsum * ROUTED_SCALE


def _sum_experts(x3):
    return jnp.sum(jnp.sum(x3, axis=0, keepdims=True), axis=1, keepdims=True)


def _two_part_specs(block, n_first, axis, rest=0):
    def index(part_index):
        def index_map(*grid_ids):
            i = grid_ids[axis]
            idx = [0] * len(block)
            idx[len(block) - 1 - rest] = part_index(i)
            return tuple(idx)
        return index_map
    return [pl.BlockSpec(block, index(lambda i: jnp.minimum(i, n_first - 1))),
            pl.BlockSpec(block, index(lambda i: jnp.maximum(i - n_first, 0)))]


def _router_kernel(lgc_ref, lgl_ref, b_ref, pos_ref, w_ref, cnt_out_ref, cnt_ref, base_ref, *, tm, n_ctx):
    phase, i = pl.program_id(0), pl.program_id(1)
    tc = lgc_ref.shape[1]
    logits = jnp.where(i < n_ctx, lgc_ref[...], lgl_ref[...])
    eid, picks, sel, comb = _route(logits, b_ref[...])
    sel = sel.astype(F32).reshape(N_EXPERTS, tc)

    @pl.when(phase == 0)
    def _count():
        @pl.when(i == 0)
        def _():
            cnt_ref[...] = jnp.zeros_like(cnt_ref)
        cnt_ref[...] += jnp.sum(sel, axis=1, keepdims=True)
        cnt_out_ref[...] = jnp.broadcast_to(cnt_ref[...], cnt_out_ref.shape)
        pos_ref[...] = jnp.zeros_like(pos_ref)
        w_ref[...] = jnp.zeros_like(w_ref)

    @pl.when(phase == 1)
    def _assign():
        @pl.when(i == 0)
        def _():
            padded = cnt_ref[...] + (tm - 1)
            tiles = jnp.floor(padded * (1.0 / tm))
            tiles = jnp.where((tiles + 1.0) * tm <= padded, tiles + 1.0, tiles)
            tiles = jnp.where(tiles * tm > padded, tiles - 1.0, tiles)
            r = lax.broadcasted_iota(I32, (N_EXPERTS, N_EXPERTS), 0)
            c = lax.broadcasted_iota(I32, (N_EXPERTS, N_EXPERTS), 1)
            lower = (c < r).astype(BF16)
            first_tile = _dot(lower, jnp.broadcast_to(tiles, (N_EXPERTS, LANES)).astype(BF16))
            base_ref[...] = first_tile[:, :1] * tm
        r = lax.broadcasted_iota(I32, (tc, tc), 0)
        c = lax.broadcasted_iota(I32, (tc, tc), 1)
        incl = _dot(sel.astype(BF16), (r <= c).astype(BF16))
        row = (base_ref[...] + incl - 1.0).reshape(eid.shape)
        base_ref[...] += jnp.sum(sel, axis=1, keepdims=True)
        pos, wts = [], []
        for pick in picks:
            hit = eid == pick
            pos.append(_sum_experts(jnp.where(hit, row, 0.0)).reshape(1, tc))
            wts.append(_sum_experts(jnp.where(hit, comb, 0.0)).reshape(1, tc))
        pos_ref[...] = jnp.concatenate(pos, axis=0).astype(I32)
        wts = jnp.concatenate(wts + [jnp.zeros((LANES - TOP_K, tc), F32)], axis=0).T
        w_ref[...] = _rows_to_slab(jnp.concatenate(
            [jnp.broadcast_to(wts[:, k:k + 1], (tc, LANES)) for k in range(TOP_K)], axis=-1))


def _router(logits_ctx, logits_lat, b_router, layer, tm, tc=1024):
    e = logits_ctx.shape[0]
    n_ctx = logits_ctx.shape[1] // tc
    t = logits_ctx.shape[1] + logits_lat.shape[1]
    return pl.pallas_call(
        functools.partial(_router_kernel, tm=tm, n_ctx=n_ctx),
        grid=(2, t // tc),
        in_specs=_two_part_specs((e, tc), n_ctx, axis=1) + [
            pl.BlockSpec((None, e, 1), lambda p, i: (layer, 0, 0))],
        out_specs=[pl.BlockSpec((TOP_K, tc), lambda p, i: (0, p * i)),
                   pl.BlockSpec((tc, TOP_K, LANES), lambda p, i: (p * i, 0, 0)),
                   pl.BlockSpec((e, LANES), lambda p, i: (0, 0))],
        out_shape=[jax.ShapeDtypeStruct((TOP_K, t), I32), jax.ShapeDtypeStruct((t, TOP_K, LANES), F32),
                   jax.ShapeDtypeStruct((e, LANES), F32)],
        scratch_shapes=[pltpu.VMEM((e, 1), F32), pltpu.VMEM((e, 1), F32)],
        compiler_params=_cparams(("arbitrary", "arbitrary")),
        name="router",
    )(logits_ctx, logits_lat, b_router.reshape(-1, e, 1))


def _tile_plan(counts, tm, n_tiles):
    counts = counts.astype(I32)
    tiles = (counts + (tm - 1)) // tm
    ends = jnp.cumsum(tiles)
    n_active = ends[-1]
    tile = jnp.arange(n_tiles, dtype=I32)
    expert = jnp.sum((tile[:, None] >= ends[None, :]).astype(I32), axis=1)
    expert = jnp.where(tile < n_active, expert, expert[n_active - 1])
    return counts, (ends - tiles) * tm, tiles, expert, n_active.reshape(1)


def _dispatch_kernel(row0_ref, tiles_ref, cnt_ref, hc_ref, hl_ref, pos_ref, xs_ref, h_ref, zero_ref, pos_smem,
                     row_sem, zero_sem, pos_sem, *, tm, n_ctx):
    tt = h_ref.shape[0]
    h = jnp.where(pl.program_id(0) < n_ctx, hc_ref[...], hl_ref[...])
    h_ref[...] = _rows_to_slab(h)

    def zero_block(e, b, action):
        start = row0_ref[e] + (tiles_ref[e] - 1) * tm + b * ZERO_ROWS

        @pl.when(jnp.logical_and(tiles_ref[e] > 0, start + ZERO_ROWS > row0_ref[e] + cnt_ref[e]))
        def _():
            action(pltpu.make_async_copy(
                zero_ref, xs_ref.at[pl.ds(pl.multiple_of(start, ZERO_ROWS), ZERO_ROWS)], zero_sem))

    def for_all_zero_blocks(action):
        def per_expert(e, carry):
            for b in range(tm // ZERO_ROWS):
                zero_block(e, b, action)
            return carry
        lax.fori_loop(0, N_EXPERTS, per_expert, 0)

    @pl.when(pl.program_id(0) == 0)
    def _zero_padding_rows():
        zero_ref[...] = jnp.zeros_like(zero_ref)
        for_all_zero_blocks(lambda copy: copy.start())
        for_all_zero_blocks(lambda copy: copy.wait())

    pos_copy = pltpu.make_async_copy(pos_ref, pos_smem, pos_sem)
    pos_copy.start()
    pos_copy.wait()

    def issue(t, carry):
        for k in range(TOP_K):
            pltpu.make_async_copy(h_ref.at[t], xs_ref.at[pos_smem[k, t]],
                                  row_sem).start(priority=k % N_DMA_PRIORITIES)
        return carry

    lax.fori_loop(0, tt, issue, 0, unroll=8)
    for _ in range(TOP_K):
        pltpu.make_async_copy(h_ref, xs_ref.at[pl.ds(0, tt)], row_sem).wait()


def _dispatch(h_ctx, h_lat, pos, first_row, tiles, counts, n_rows, tm, tt=DISPATCH_TT):
    d = h_ctx.shape[1]
    n_ctx = h_ctx.shape[0] // tt
    t = h_ctx.shape[0] + h_lat.shape[0]
    assert tm % ZERO_ROWS == 0
    return pl.pallas_call(
        functools.partial(_dispatch_kernel, tm=tm, n_ctx=n_ctx),
        grid_spec=pltpu.PrefetchScalarGridSpec(
            num_scalar_prefetch=3,
            grid=(t // tt,),
            in_specs=_two_part_specs((tt, d), n_ctx, axis=0, rest=1) + [
                pl.BlockSpec((TOP_K, tt), lambda i, *_: (0, i))],
            out_specs=pl.BlockSpec(memory_space=pl.ANY),
            scratch_shapes=[pltpu.VMEM((tt, d // LANES, LANES), BF16),
                            pltpu.VMEM((ZERO_ROWS, d // LANES, LANES), BF16),
                            pltpu.SMEM((TOP_K, tt), I32),
                            pltpu.SemaphoreType.DMA, pltpu.SemaphoreType.DMA, pltpu.SemaphoreType.DMA],
        ),
        out_shape=jax.ShapeDtypeStruct((n_rows, d // LANES, LANES), BF16),
        compiler_params=_cparams(("arbitrary",)),
        name="moe_dispatch",
    )(first_row, tiles, counts, h_ctx, h_lat, pos)


def _expert_kernel(te_ref, na_ref, xs_ref, wg_ref, wu_ref, wd_ref, o_ref, wgu_bf, wd_bf):
    i = pl.program_id(0)
    f = wg_ref.shape[1]

    @pl.when(i < na_ref[0])
    def _():
        @pl.when(jnp.logical_or(i == 0, te_ref[i] != te_ref[jnp.maximum(i - 1, 0)]))
        def _():
            wgu_bf[:, :f] = wg_ref[...].astype(BF16)
            wgu_bf[:, f:] = wu_ref[...].astype(BF16)
            wd_bf[...] = wd_ref[...].astype(BF16)

        x = _slab_to_rows(xs_ref[...])
        hgu = _dot(x, wgu_bf[...])
        out = _dot((_silu(hgu[:, :f]) * hgu[:, f:]).astype(BF16), wd_bf[...])
        o_ref[...] = _rows_to_slab(out.astype(BF16))


def _experts(xs, tile_expert, n_active, w_gate, w_up, w_down, layer, tm):
    n_rows, s, _ = xs.shape
    d, f = w_gate.shape[-2:]
    row = lambda i, te, na: (jnp.minimum(i, na[0] - 1), 0, 0)
    wspec = lambda shape: pl.BlockSpec((None, None) + shape, lambda i, te, na: (layer, te[i], 0, 0))
    return pl.pallas_call(
        _expert_kernel,
        grid_spec=pltpu.PrefetchScalarGridSpec(
            num_scalar_prefetch=2,
            grid=(n_rows // tm,),
            in_specs=[pl.BlockSpec((tm, s, LANES), row), wspec((d, f)), wspec((d, f)), wspec((f, d))],
            out_specs=pl.BlockSpec((tm, s, LANES), row),
            scratch_shapes=[pltpu.VMEM((d, 2 * f), BF16), pltpu.VMEM((f, d), BF16)],
        ),
        out_shape=jax.ShapeDtypeStruct((n_rows, s, LANES), BF16),
        compiler_params=_cparams(("arbitrary",)),
        name="moe_experts",
    )(tile_expert, n_active, xs, w_gate, w_up, w_down)


def _combine_kernel(pos_ref, w_ref, h_ref, x_ref, gt2_ref, sg_ref, su_ref, sd_ref, gfin_ref, ys_ref,
                    o_ref, buf_ref, pos_smem, row_sem, pos_sem, *, final):
    tt = x_ref.shape[0]
    pos_copy = pltpu.make_async_copy(pos_ref, pos_smem, pos_sem)
    pos_copy.start()
    pos_copy.wait()

    def issue(t, carry):
        for k in range(TOP_K):
            pltpu.make_async_copy(ys_ref.at[pos_smem[k, t]], buf_ref.at[k, t],
                                  row_sem).start(priority=k % N_DMA_PRIORITIES)
        return carry

    lax.fori_loop(0, tt, issue, 0, unroll=8)
    h = h_ref[...]
    shared = _dot((_silu(_dot(h, sg_ref[...])) * _dot(h, su_ref[...])).astype(BF16), sd_ref[...])
    for k in range(TOP_K):
        pltpu.make_async_copy(ys_ref.at[pl.ds(0, tt)], buf_ref.at[k], row_sem).wait()
    routed = buf_ref[0].astype(F32) * w_ref[:, 0:1, :]
    for k in range(1, TOP_K):
        routed = routed + buf_ref[k].astype(F32) * w_ref[:, k:k + 1, :]
    y = x_ref[...] + gt2_ref[...] * (_slab_to_rows(routed) + shared)
    o_ref[...] = _rms(y, gfin_ref[...]) if final else y


def _combine(pos, w, h2, x, mod5, ys, p, g_final, layer, row_fn, row0, final, tt=COMBINE_TT):
    t, d = x.shape
    f = p["ws_gate"].shape[-1]
    tile0 = row0 // tt
    tok = lambda i: (i, 0)
    lyr = lambda shape: _resident((None,) + shape, lambda i: (layer,) + (0,) * len(shape))
    return pl.pallas_call(
        functools.partial(_combine_kernel, final=final),
        grid=(t // tt,),
        in_specs=[
            pl.BlockSpec((TOP_K, tt), lambda i: (0, tile0 + i)),
            pl.BlockSpec((tt, TOP_K, LANES), lambda i: (tile0 + i, 0, 0)),
            pl.BlockSpec((tt, d), tok), pl.BlockSpec((tt, d), tok),
            pl.BlockSpec((None, None, None, 1, d), lambda i: (layer, row_fn(i), 5, 0, 0)),
            lyr((d, f)), lyr((d, f)), lyr((f, d)),
            _resident((1, d), lambda i: (0, 0)),
            pl.BlockSpec(memory_space=pl.ANY),
        ],
        out_specs=pl.BlockSpec((tt, d), tok),
        out_shape=jax.ShapeDtypeStruct((t, d), F32),
        scratch_shapes=[pltpu.VMEM((TOP_K, tt, d // LANES, LANES), BF16), pltpu.SMEM((TOP_K, tt), I32),
                        pltpu.SemaphoreType.DMA, pltpu.SemaphoreType.DMA],
        compiler_params=_cparams(("arbitrary",)),
        name="moe_combine",
    )(pos, w, h2, x, mod5, p["ws_gate"], p["ws_up"], p["ws_down"], g_final.reshape(1, d), ys)


def _rope_tables(n_tokens, dim):
    rows = n_tokens // GRID_W
    row = jnp.repeat(jnp.arange(rows, dtype=jnp.int32), GRID_W).astype(F32)
    col = jnp.tile(jnp.arange(GRID_W, dtype=jnp.int32), rows).astype(F32)
    half = dim // 2
    inv_freq = ROPE_THETA ** (-(jnp.arange(half // 2, dtype=F32) * 2.0 / half))
    ang_r = row[:, None] * inv_freq[None, :]
    ang_c = col[:, None] * inv_freq[None, :]
    cos = jnp.concatenate([jnp.cos(ang_r)] * 2 + [jnp.cos(ang_c)] * 2, axis=-1)
    sin = jnp.concatenate([-jnp.sin(ang_r), jnp.sin(ang_r), -jnp.sin(ang_c), jnp.sin(ang_c)], axis=-1)
    pad = ((0, 0), (0, LANES - dim))
    return jnp.pad(cos, pad), jnp.pad(sin, pad)


def _prep_params(w_in, w_uq, w_ukv, w_out, w_router, w_gate, w_up, w_down, ws_gate, ws_up, ws_down):
    depth = w_in.shape[0]
    w_in_p = jnp.pad(w_in.astype(BF16), ((0, 0), (0, 0), (0, W_IN_PAD - w_in.shape[-1])))
    w_uq_r = jnp.pad(w_uq.reshape(depth, MLA_Q_RANK, MLA_HEADS, MLA_NOPE + MLA_ROPE),
                     ((0, 0), (0, 0), (0, 0), (0, MLA_QH - MLA_NOPE - MLA_ROPE)))
    w_uq_r = w_uq_r.reshape(depth, MLA_Q_RANK, MLA_HEADS * MLA_QH).astype(BF16)
    return {
        "w_in": w_in_p, "w_uq": w_uq_r, "w_ukv": w_ukv.astype(BF16), "w_out": w_out.astype(BF16),
        "w_router_t": jnp.swapaxes(w_router, 1, 2),
        "w_gate": w_gate, "w_up": w_up, "w_down": w_down,
        "ws_gate": ws_gate.astype(BF16), "ws_up": ws_up.astype(BF16), "ws_down": ws_down.astype(BF16),
    }


def kernel(x_prompt, x_sample, c, cache_gqa_k, cache_gqa_v, cache_mla_ckv, cache_mla_kpe, c_ctx, w_mod, b_mod, g_mix, w_in, g_q, g_k, conv_w, g_mla_q, g_mla_kv, w_uq, w_ukv, g_grp, w_out, g_ffn, w_router, b_router, w_gate, w_up, w_down, ws_gate, ws_up, ws_down, g_final):
    batch, seq, d = x_prompt.shape
    dec_batch, dec_seq, _ = x_sample.shape
    depth = w_mod.shape[0]
    past = cache_gqa_k.shape[2]
    t_ctx, t_lat = batch * seq, dec_batch * dec_seq
    n_moe_tiles = pl.cdiv((t_ctx + t_lat) * TOP_K, MOE_TM) + N_EXPERTS

    p = _prep_params(w_in, w_uq, w_ukv, w_out, w_router, w_gate, w_up, w_down, ws_gate, ws_up, ws_down)
    for name, val in (("g_q", g_q), ("g_k", g_k), ("g_mla_q", g_mla_q), ("g_mla_kv", g_mla_kv),
                      ("g_grp", g_grp), ("g_ffn", g_ffn)):
        p[name] = val.reshape(depth, 1, -1)
    p["conv_w"] = conv_w

    mod_rows = 8
    cond = jnp.zeros((mod_rows, d), F32).at[0].set(c_ctx).at[1:1 + dec_batch].set(c)
    mod5 = _modulation(cond, w_mod, b_mod).reshape(depth, mod_rows, 6, 1, d)
    ctx_row = lambda i: 0
    lat_row = lambda tile: (lambda i: 1 + i // (dec_seq // tile))

    cache = (cache_gqa_k.reshape(dec_batch, depth, past, KV_W),
             cache_gqa_v.reshape(dec_batch, depth, past, KV_W), cache_mla_ckv, cache_mla_kpe)
    rope = _rope_tables(dec_seq, HEAD_DIM) + _rope_tables(dec_seq, MLA_ROPE)

    xc, xl = x_prompt.reshape(t_ctx, d), x_sample.reshape(t_lat, d)
    new_ctx = []
    for layer in range(depth):
        final = layer == depth - 1
        zqc, zsc = _in_proj(xc, mod5, g_mix, p["w_in"], layer, ctx_row, IN_TM)
        zql, zsl = _in_proj(xl, mod5, g_mix, p["w_in"], layer, lat_row(IN_TM), IN_TM)
        xc, h2c, lgc, nk, nv, nckv, nkpe = _mixer(
            zqc, zsc, xc, mod5, p, layer, n_seq=batch, seq=seq, mod_row_fn=ctx_row)
        xl, h2l, lgl = _mixer(
            zql, zsl, xl, mod5, p, layer, n_seq=dec_batch, seq=dec_seq,
            mod_row_fn=lambda b: 1 + b, cache=cache, rope=rope, qb=128)
        new_ctx.append((nk, nv, nckv, nkpe))
        pos, w_tok, counts = _router(lgc, lgl, b_router, layer, MOE_TM)
        counts, first_row, tiles, tile_expert, n_active = _tile_plan(counts[:, 0], MOE_TM, n_moe_tiles)
        xs = _dispatch(h2c, h2l, pos, first_row, tiles, counts, n_moe_tiles * MOE_TM, MOE_TM)
        ys = _experts(xs, tile_expert, n_active, p["w_gate"], p["w_up"], p["w_down"], layer, MOE_TM)
        xc = _combine(pos, w_tok, h2c, xc, mod5, ys, p, g_final, layer, ctx_row, 0, final)
        xl = _combine(pos, w_tok, h2l, xl, mod5, ys, p, g_final, layer, lat_row(COMBINE_TT), t_ctx, final)
    stack = lambda i: jnp.stack([lc[i] for lc in new_ctx], axis=1)
    new_k = stack(0).reshape(batch, depth, seq, GQA_KV_HEADS, HEAD_DIM)
    new_v = stack(1).reshape(batch, depth, seq, GQA_KV_HEADS, HEAD_DIM)
    return (xc.reshape(batch, seq, d), xl.reshape(dec_batch, dec_seq, d), new_k, new_v, stack(2), stack(3))
```
